```python
import math
import jax, jax.numpy as jnp
from jax import lax
import numpy as np

D_MODEL = 1024
BATCH = 8
SEQ = 4096
DEPTH = 1

SSM_GROUPS = 32
SSM_GROUP_CH = 16
SSM_WIDTH = SSM_GROUPS * SSM_GROUP_CH
SSM_STATE = 64
DT_MIN = 1e-3
DT_MAX = 1e-1
N_HEADS = 8
QK_NOPE = 128
QK_ROPE = 64
QK_HEAD = QK_NOPE + QK_ROPE
V_HEAD = 128
Q_LORA = 384
KV_LORA = 256
ROPE_THETA = 10000.0
Q_BLOCK = 128
MAX_POS_OFFSET = 1024
D_FF = 4 * D_MODEL
EPS = 1e-6
IN_SIZES = (SSM_WIDTH, Q_LORA, KV_LORA + QK_ROPE, D_MODEL, D_MODEL)
IN_OFFSETS = tuple(int(v) for v in np.cumsum(IN_SIZES)[:-1])
D_IN = sum(IN_SIZES)

kernel_name = "hybrid_s5_mla_gated_block"


def rms_norm(x, gain):
    xf = x.astype(jnp.float32)
    inv = lax.rsqrt(jnp.mean(xf * xf, axis=-1, keepdims=True) + EPS)
    return (xf * inv * gain.astype(jnp.float32)).astype(x.dtype)


def rope_tables(positions):
    half = QK_ROPE // 2
    inv_freq = ROPE_THETA ** (-jnp.arange(half, dtype=jnp.float32) / half)
    ang = positions.astype(jnp.float32)[..., None] * inv_freq
    return jnp.cos(ang)[:, :, None, :], jnp.sin(ang)[:, :, None, :]


def apply_rope(x, cos, sin):
    xf = x.astype(jnp.float32)
    x1, x2 = jnp.split(xf, 2, axis=-1)
    out = jnp.concatenate([x1 * cos - x2 * sin, x2 * cos + x1 * sin], axis=-1)
    return out.astype(x.dtype)


def causal_block_attention(q, k, v):
    b, l, h, dq = q.shape
    nblk = l // Q_BLOCK
    qb = q.reshape(b, nblk, Q_BLOCK, h, dq).transpose(1, 0, 2, 3, 4)
    key_idx = jnp.arange(l)
    scale = QK_HEAD ** -0.5

    def one_block(args):
        q_blk, blk = args
        s = jnp.einsum('bqhd,bkhd->bhqk', q_blk, k, preferred_element_type=jnp.float32) * scale
        q_idx = blk * Q_BLOCK + jnp.arange(Q_BLOCK)
        s = jnp.where(key_idx[None, :] <= q_idx[:, None], s, -jnp.inf)
        p = jax.nn.softmax(s, axis=-1).astype(v.dtype)
        return jnp.einsum('bhqk,bkhd->bqhd', p, v)

    out = lax.map(one_block, (qb, jnp.arange(nblk)))
    return out.transpose(1, 0, 2, 3, 4).reshape(b, l, h, -1)


def s5_ssm(u, a_re, a_im, log_dt, b_re, b_im, c_re, c_im, d_skip):
    f32 = jnp.float32
    dt = jnp.exp(log_dt.astype(f32))[:, None]
    lam = lax.complex(a_re.astype(f32), a_im.astype(f32))
    lam_bar = jnp.exp(lam * dt)
    b_mat = lax.complex(b_re.astype(f32), b_im.astype(f32))
    b_bar = ((lam_bar - 1.0) / lam)[..., None] * b_mat
    bu = jnp.einsum('gpc,blgc->blgp', b_bar, u.astype(f32).astype(jnp.complex64))
    a = jnp.broadcast_to(lam_bar, (1, u.shape[1]) + lam_bar.shape)

    def combine(left, right):
        a_l, b_l = left
        a_r, b_r = right
        return a_r * a_l, a_r * b_l + b_r

    _, states = lax.associative_scan(combine, (a, bu), axis=1)
    c_mat = lax.complex(c_re.astype(f32), c_im.astype(f32))
    y = jnp.real(jnp.einsum('gcp,blgp->blgc', c_mat, states)) + d_skip.astype(f32) * u.astype(f32)
    return y.astype(u.dtype)


def hybrid_layer(x, positions, norm_mix, w_in, q_a_norm, kv_a_norm, w_q_b, w_kv_b, q_norm, k_norm, w_o_mla,
                 ssm_a_re, ssm_a_im, ssm_log_dt, ssm_b_re, ssm_b_im, ssm_c_re, ssm_c_im, ssm_d,
                 w_glu, b_glu, w_o_ssm, w_out, norm_mlp, w_up, w_down):
    b, l, _ = x.shape
    xn = rms_norm(x, norm_mix)
    proj = xn @ w_in
    u, q_lat, kv_lat, gate_ssm, gate_mla = jnp.split(proj, IN_OFFSETS, axis=-1)

    y = s5_ssm(u.reshape(b, l, SSM_GROUPS, SSM_GROUP_CH), ssm_a_re, ssm_a_im, ssm_log_dt,
               ssm_b_re, ssm_b_im, ssm_c_re, ssm_c_im, ssm_d).reshape(b, l, SSM_WIDTH)
    z = jax.nn.gelu(y)
    z = z * jax.nn.sigmoid(z @ w_glu + b_glu)
    y_ssm = z @ w_o_ssm

    q = (rms_norm(q_lat, q_a_norm) @ w_q_b).reshape(b, l, N_HEADS, QK_HEAD)
    c_kv, k_pe = kv_lat[..., :KV_LORA], kv_lat[..., KV_LORA:]
    kv = (rms_norm(c_kv, kv_a_norm) @ w_kv_b).reshape(b, l, N_HEADS, QK_NOPE + V_HEAD)
    k_nope, v = kv[..., :QK_NOPE], kv[..., QK_NOPE:]
    k = jnp.concatenate([k_nope, jnp.broadcast_to(k_pe[:, :, None, :], (b, l, N_HEADS, QK_ROPE))], axis=-1)
    q = rms_norm(q, q_norm)
    k = rms_norm(k, k_norm)
    cos, sin = rope_tables(positions)
    q = jnp.concatenate([q[..., :QK_NOPE], apply_rope(q[..., QK_NOPE:], cos, sin)], axis=-1)
    k = jnp.concatenate([k[..., :QK_NOPE], apply_rope(k[..., QK_NOPE:], cos, sin)], axis=-1)
    attn = causal_block_attention(q, k, v).reshape(b, l, N_HEADS * V_HEAD)
    y_mla = attn @ w_o_mla

    mixed = jax.nn.sigmoid(gate_ssm) * y_ssm + jax.nn.sigmoid(gate_mla) * y_mla
    h = x + mixed @ w_out

    hidden = jnp.square(jax.nn.relu(rms_norm(h, norm_mlp) @ w_up))
    return h + hidden @ w_down


def setup_inputs(seed: int = 0) -> dict:
    key = jax.random.key(seed)
    ks = jax.random.split(key, 32)
    f32 = jnp.float32

    def dense(k, fan_in, shape):
        return jax.random.normal(k, (DEPTH,) + shape, f32) * (fan_in ** -0.5)

    def gain(k, n):
        return 1.0 + 0.02 * jax.random.normal(k, (DEPTH, n), f32)

    x = jax.random.normal(ks[0], (BATCH, SEQ, D_MODEL), f32)
    offset = jax.random.randint(ks[1], (BATCH, 1), 0, MAX_POS_OFFSET, dtype=jnp.int32)
    positions = (offset + jnp.arange(SEQ, dtype=jnp.int32)[None, :]).astype(jnp.int32)
    n_idx = jnp.arange(SSM_STATE, dtype=f32)
    ssm_a_re = -0.5 + 0.01 * jax.random.normal(ks[2], (DEPTH, SSM_GROUPS, SSM_STATE), f32)
    ssm_a_im = math.pi * n_idx[None, None, :] + 0.01 * jax.random.normal(ks[3], (DEPTH, SSM_GROUPS, SSM_STATE), f32)
    ssm_log_dt = jax.random.uniform(ks[4], (DEPTH, SSM_GROUPS), f32, math.log(DT_MIN), math.log(DT_MAX))
    return {
        "x": x,
        "positions": positions,
        "norm_mix": gain(ks[5], D_MODEL),
        "w_in": dense(ks[6], D_MODEL, (D_MODEL, D_IN)),
        "q_a_norm": gain(ks[7], Q_LORA),
        "kv_a_norm": gain(ks[8], KV_LORA),
        "w_q_b": dense(ks[9], Q_LORA, (Q_LORA, N_HEADS * QK_HEAD)),
        "w_kv_b": dense(ks[10], KV_LORA, (KV_LORA, N_HEADS * (QK_NOPE + V_HEAD))),
        "q_norm": gain(ks[11], QK_HEAD),
        "k_norm": gain(ks[12], QK_HEAD),
        "w_o_mla": dense(ks[13], N_HEADS * V_HEAD, (N_HEADS * V_HEAD, D_MODEL)),
        "ssm_a_re": ssm_a_re,
        "ssm_a_im": ssm_a_im,
        "ssm_log_dt": ssm_log_dt,
        "ssm_b_re": dense(ks[14], 2 * SSM_GROUP_CH, (SSM_GROUPS, SSM_STATE, SSM_GROUP_CH)),
        "ssm_b_im": dense(ks[15], 2 * SSM_GROUP_CH, (SSM_GROUPS, SSM_STATE, SSM_GROUP_CH)),
        "ssm_c_re": dense(ks[16], 2 * SSM_STATE, (SSM_GROUPS, SSM_GROUP_CH, SSM_STATE)),
        "ssm_c_im": dense(ks[17], 2 * SSM_STATE, (SSM_GROUPS, SSM_GROUP_CH, SSM_STATE)),
        "ssm_d": jax.random.normal(ks[18], (DEPTH, SSM_GROUPS, SSM_GROUP_CH), f32),
        "w_glu": dense(ks[19], SSM_WIDTH, (SSM_WIDTH, SSM_WIDTH)),
        "b_glu": 0.01 * jax.random.normal(ks[20], (DEPTH, SSM_WIDTH), f32),
        "w_o_ssm": dense(ks[21], SSM_WIDTH, (SSM_WIDTH, D_MODEL)),
        "w_out": dense(ks[22], D_MODEL, (D_MODEL, D_MODEL)),
        "norm_mlp": gain(ks[23], D_MODEL),
        "w_up": dense(ks[24], D_MODEL, (D_MODEL, D_FF)),
        "w_down": dense(ks[25], D_FF, (D_FF, D_MODEL)),
    }


def reference(x, positions, norm_mix, w_in, q_a_norm, kv_a_norm, w_q_b, w_kv_b, q_norm, k_norm, w_o_mla,
              ssm_a_re, ssm_a_im, ssm_log_dt, ssm_b_re, ssm_b_im, ssm_c_re, ssm_c_im, ssm_d,
              w_glu, b_glu, w_o_ssm, w_out, norm_mlp, w_up, w_down):
    h = x
    for layer in range(DEPTH):
        h = hybrid_layer(h, positions, norm_mix[layer], w_in[layer], q_a_norm[layer], kv_a_norm[layer],
                         w_q_b[layer], w_kv_b[layer], q_norm[layer], k_norm[layer], w_o_mla[layer],
                         ssm_a_re[layer], ssm_a_im[layer], ssm_log_dt[layer], ssm_b_re[layer], ssm_b_im[layer],
                         ssm_c_re[layer], ssm_c_im[layer], ssm_d[layer], w_glu[layer], b_glu[layer],
                         w_o_ssm[layer], w_out[layer], norm_mlp[layer], w_up[layer], w_down[layer])
    return h
```

```python
import functools
import math

import numpy as np
import jax
import jax.numpy as jnp
from jax import lax
from jax.experimental import pallas as pl
from jax.experimental.pallas import tpu as pltpu

D_MODEL = 1024
SSM_GROUPS = 32
SSM_GROUP_CH = 16
SSM_WIDTH = SSM_GROUPS * SSM_GROUP_CH
SSM_STATE = 64
N_HEADS = 8
N_PAIRS = N_HEADS // 2
QK_NOPE = 128
QK_ROPE = 64
QK_HEAD = QK_NOPE + QK_ROPE
V_HEAD = 128
Q_LORA = 384
KV_LORA = 256
ROPE_THETA = 10000.0
D_FF = 4 * D_MODEL
EPS = 1e-6

LANES = 128
SSM_CHUNK = 16
SSM_ROW = SSM_CHUNK * SSM_GROUP_CH
HEAD_PAD = 256
PAIR_W = 2 * QK_NOPE + LANES
Q_MAIN = N_PAIRS * PAIR_W
Q_AUX = N_PAIRS * LANES
D_IN2 = SSM_WIDTH + Q_LORA + KV_LORA + 2 * LANES + 2 * D_MODEL
VMEM_LIMIT = 56 * 1024 * 1024

TM_PROJ = 512
TM_POST = 512
TQ = 512
TK = 512
FF_CHUNK = 1024

F32 = jnp.float32
BF16 = jnp.bfloat16


def _dot(a, b):
    return jnp.dot(a, b, preferred_element_type=F32)


def _rms(x, gain):
    ms = jnp.mean(x * x, axis=-1, keepdims=True)
    return x * lax.rsqrt(ms + EPS) * gain


def _rope_kernel(pos_ref, freq_ref, cos_ref, sin_ref):
    ang = freq_ref[...] * pos_ref[0].astype(F32)
    cos_ref[0] = jnp.cos(ang)
    sin_ref[0] = jnp.sin(ang)


def _rope_tables(positions):
    b, l = positions.shape
    half = QK_ROPE // 2
    inv_freq = ROPE_THETA ** (-jnp.arange(half, dtype=F32) / half)
    out = jax.ShapeDtypeStruct((b, half, l), F32)
    cos_t, sin_t = pl.pallas_call(
        _rope_kernel,
        grid=(b,),
        in_specs=[pl.BlockSpec((1, 1, l), lambda i: (i, 0, 0)),
                  pl.BlockSpec((half, 1), lambda i: (0, 0))],
        out_specs=[pl.BlockSpec((1, half, l), lambda i: (i, 0, 0))] * 2,
        out_shape=[out, out],
        name="rope_tables",
    )(positions.reshape(b, 1, l), inv_freq.reshape(half, 1))

    def expand(t):
        t = jnp.transpose(t, (0, 2, 1))
        return jnp.tile(t, (1, 1, LANES // half)).reshape(b * l, LANES)

    return expand(cos_t), expand(sin_t)


def _in_proj_kernel(x_ref, cos_ref, sin_ref, nmix_ref, w_in_ref, qan_ref, kvan_ref, wq_ref, wkv_ref,
                    gqm_ref, gqa_ref, gkn_ref, gkx_ref, gkr_ref, eq_ref, ek_ref, etq_ref, etk_ref,
                    u_ref, q_ref, k_ref, v_ref, sgs_ref, sgm_ref):
    o_q = SSM_WIDTH
    o_kv = o_q + Q_LORA
    o_kp = o_kv + KV_LORA
    o_kr = o_kp + LANES
    o_gs = o_kr + LANES
    o_gm = o_gs + D_MODEL

    xn = _rms(x_ref[...], nmix_ref[...]).astype(BF16)
    cos = cos_ref[...]
    sin = sin_ref[...]

    u_ref[...] = _dot(xn, w_in_ref[:, 0:o_q]).astype(BF16)
    sgs_ref[...] = jax.nn.sigmoid(_dot(xn, w_in_ref[:, o_gs:o_gm])).astype(BF16)
    sgm_ref[...] = jax.nn.sigmoid(_dot(xn, w_in_ref[:, o_gm:o_gm + D_MODEL])).astype(BF16)

    def head_scales(sq, e_ref, et_ref):
        sums = _dot(sq.astype(BF16), e_ref[...])
        inv = lax.rsqrt(sums * (1.0 / QK_HEAD) + EPS)
        hi = inv.astype(BF16)
        lo = (inv - hi.astype(F32)).astype(BF16)
        return _dot(jnp.concatenate([hi, lo], axis=1), et_ref[...])

    qn = _rms(_dot(xn, w_in_ref[:, o_q:o_kv]), qan_ref[...]).astype(BF16)
    qraw = _dot(qn, wq_ref[...])
    qmain = qraw[:, 0:Q_MAIN]
    qs = qraw * head_scales(qmain * qmain, eq_ref, etq_ref)
    for p in range(N_PAIRS):
        lo_, mid, hi_ = p * PAIR_W, p * PAIR_W + 2 * QK_NOPE, (p + 1) * PAIR_W
        q_ref[:, lo_:mid] = (qs[:, lo_:mid] * gqm_ref[:, lo_:mid]).astype(BF16)
        xr_lo = Q_MAIN + p * LANES
        rope = (qs[:, mid:hi_] * gqm_ref[:, mid:hi_]) * cos \
            + (qs[:, xr_lo:xr_lo + LANES] * gqa_ref[:, p * LANES:(p + 1) * LANES]) * sin
        q_ref[:, mid:hi_] = rope.astype(BF16)

    ckv = _rms(_dot(xn, w_in_ref[:, o_kv:o_kp]), kvan_ref[...]).astype(BF16)
    kvraw = _dot(ckv, wkv_ref[...])
    knope = kvraw[:, 0:N_HEADS * QK_NOPE]
    v_ref[...] = kvraw[:, N_HEADS * QK_NOPE:].astype(BF16)
    kpe = _dot(xn, w_in_ref[:, o_kp:o_kr])
    kpe_rot = _dot(xn, w_in_ref[:, o_kr:o_gs])
    ksc = head_scales(jnp.concatenate([knope * knope, kpe * kpe], axis=1), ek_ref, etk_ref)
    krope = (kpe * gkx_ref[...]) * cos + (kpe_rot * gkr_ref[...]) * sin
    kn = knope * ksc[:, 0:N_HEADS * QK_NOPE] * gkn_ref[...]
    for h in range(N_HEADS):
        k_ref[0, h, :, 0:QK_NOPE] = kn[:, h * QK_NOPE:(h + 1) * QK_NOPE].astype(BF16)
        off = N_HEADS * QK_NOPE + h * LANES
        k_ref[0, h, :, QK_NOPE:HEAD_PAD] = (krope * ksc[:, off:off + LANES]).astype(BF16)


def _const_spec(shape):
    nd = len(shape)
    return pl.BlockSpec(shape, lambda *_: (0,) * nd, pipeline_mode=pl.Buffered(1))


def _in_proj(x2d, cos, sin, prm, batch, seq):
    t = x2d.shape[0]
    tm = TM_PROJ
    nlt = seq // tm
    row = lambda w: pl.BlockSpec((tm, w), lambda i: (i, 0))
    consts = [prm["nmix"], prm["w_in"], prm["qan"], prm["kvan"], prm["wq"], prm["wkv"],
              prm["gqm"], prm["gqa"], prm["gkn"], prm["gkx"], prm["gkr"],
              prm["eq"], prm["ek"], prm["etq"], prm["etk"]]
    out_shape = [
        jax.ShapeDtypeStruct((t, SSM_WIDTH), BF16),
        jax.ShapeDtypeStruct((t, Q_MAIN), BF16),
        jax.ShapeDtypeStruct((batch, N_HEADS, seq, HEAD_PAD), BF16),
        jax.ShapeDtypeStruct((t, N_HEADS * V_HEAD), BF16),
        jax.ShapeDtypeStruct((t, D_MODEL), BF16),
        jax.ShapeDtypeStruct((t, D_MODEL), BF16),
    ]
    out_specs = [
        row(SSM_WIDTH), row(Q_MAIN),
        pl.BlockSpec((1, N_HEADS, tm, HEAD_PAD), lambda i: (i // nlt, 0, i % nlt, 0)),
        row(N_HEADS * V_HEAD), row(D_MODEL), row(D_MODEL),
    ]
    return pl.pallas_call(
        _in_proj_kernel,
        grid=(t // tm,),
        in_specs=[row(D_MODEL), row(LANES), row(LANES)] + [_const_spec(c.shape) for c in consts],
        out_specs=out_specs,
        out_shape=out_shape,
        compiler_params=pltpu.CompilerParams(dimension_semantics=("arbitrary",),
                                             vmem_limit_bytes=VMEM_LIMIT),
        name="in_proj",
    )(x2d, cos, sin, *consts)


def _ssm_kernel(u_ref, rhs_ref, a_ref, wout_ref, y_ref, p_sc, h_sc, *, n_chunks, batch):
    p_sc[...] = _dot(u_ref[0], rhs_ref[0])
    a_re = jnp.broadcast_to(a_ref[0, 0:1, :], (batch, LANES))
    a_im = jnp.broadcast_to(a_ref[0, 1:2, :], (batch, LANES))

    def body(k, carry):
        h_re, h_im = carry
        r0 = pl.multiple_of(k * batch, batch)
        h_sc[pl.ds(r0, batch), 0:LANES] = h_re
        h_sc[pl.ds(r0, batch), LANES:2 * LANES] = h_im
        s_re = p_sc[pl.ds(r0, batch), SSM_ROW:SSM_ROW + LANES]
        s_im = p_sc[pl.ds(r0, batch), SSM_ROW + LANES:SSM_ROW + 2 * LANES]
        return (a_re * h_re - a_im * h_im + s_re, a_re * h_im + a_im * h_re + s_im)

    zero = jnp.zeros((batch, LANES), F32)
    lax.fori_loop(0, n_chunks, body, (zero, zero), unroll=8)
    y = p_sc[:, 0:SSM_ROW] + _dot(h_sc[...].astype(BF16), wout_ref[0])
    y_ref[0] = y


def _ssm(u_g, prm, batch, n_chunks):
    rows = u_g.shape[1]
    kern = functools.partial(_ssm_kernel, n_chunks=n_chunks, batch=batch)
    return pl.pallas_call(
        kern,
        grid=(SSM_GROUPS,),
        in_specs=[pl.BlockSpec((1, rows, SSM_ROW), lambda g: (g, 0, 0)),
                  pl.BlockSpec((1, SSM_ROW, 2 * SSM_ROW), lambda g: (g, 0, 0)),
                  pl.BlockSpec((1, 8, LANES), lambda g: (g, 0, 0)),
                  pl.BlockSpec((1, SSM_ROW, SSM_ROW), lambda g: (g, 0, 0))],
        out_specs=pl.BlockSpec((1, rows, SSM_ROW), lambda g: (g, 0, 0)),
        out_shape=jax.ShapeDtypeStruct((SSM_GROUPS, rows, SSM_ROW), F32),
        scratch_shapes=[pltpu.VMEM((rows, 2 * SSM_ROW), F32), pltpu.VMEM((rows, SSM_ROW), F32)],
        compiler_params=pltpu.CompilerParams(dimension_semantics=("arbitrary",),
                                             vmem_limit_bytes=VMEM_LIMIT),
        name="ssm_scan",
    )(u_g, prm["ssm_rhs"], prm["ssm_a"], prm["ssm_wout"])


def _attn_kernel(q_ref, k_ref, v_ref, o_ref, m_sc, acc_sc):
    i = pl.program_id(2)
    ones = jnp.ones((TK, LANES), BF16)
    nt = (((1,), (1,)), ((), ()))

    def block(hh, q, kv0, masked):
        k = k_ref[0, hh, pl.ds(kv0, TK), :]
        s = lax.dot_general(q, k, nt, preferred_element_type=F32)
        if masked:
            r = lax.broadcasted_iota(jnp.int32, (TQ, TK), 0)
            c = lax.broadcasted_iota(jnp.int32, (TQ, TK), 1)
            s = jnp.where(c <= r, s, -jnp.inf)
        m_prev = m_sc[hh]
        m_new = jnp.maximum(m_prev, jnp.max(s, axis=-1, keepdims=True))
        alpha = jnp.exp2(m_prev - m_new)
        p = jnp.exp2(s - jnp.tile(m_new, (1, TK // LANES))).astype(BF16)
        v_ext = jnp.concatenate([v_ref[0, pl.ds(kv0, TK), hh * V_HEAD:(hh + 1) * V_HEAD], ones], axis=1)
        acc_sc[hh] = acc_sc[hh] * jnp.tile(alpha, (1, 2)) + _dot(p, v_ext)
        m_sc[hh] = m_new

    m_sc[...] = jnp.full(m_sc.shape, -jnp.inf, F32)
    acc_sc[...] = jnp.zeros(acc_sc.shape, F32)
    qs = [jnp.concatenate([q_ref[0, :, hh * QK_NOPE:(hh + 1) * QK_NOPE], q_ref[0, :, 2 * QK_NOPE:PAIR_W]], axis=1)
          for hh in range(2)]

    def body(j, carry):
        kv0 = pl.multiple_of(j * TK, TK)
        for hh in range(2):
            block(hh, qs[hh], kv0, False)
        return carry

    lax.fori_loop(0, i, body, 0)
    kv_diag = pl.multiple_of(i * TK, TK)
    for hh in range(2):
        block(hh, qs[hh], kv_diag, True)
        acc = acc_sc[hh]
        o_ref[0, :, hh * V_HEAD:(hh + 1) * V_HEAD] = (acc[:, 0:V_HEAD] / acc[:, V_HEAD:2 * V_HEAD]).astype(BF16)


def _attention(q, k, v, batch, seq):
    assert TQ == TK
    return pl.pallas_call(
        _attn_kernel,
        grid=(batch, N_PAIRS, seq // TQ),
        in_specs=[pl.BlockSpec((1, TQ, PAIR_W), lambda b, p, i: (b, i, p)),
                  pl.BlockSpec((1, 2, seq, HEAD_PAD), lambda b, p, i: (b, p, 0, 0)),
                  pl.BlockSpec((1, seq, 2 * V_HEAD), lambda b, p, i: (b, 0, p))],
        out_specs=pl.BlockSpec((1, TQ, 2 * V_HEAD), lambda b, p, i: (b, i, p)),
        out_shape=jax.ShapeDtypeStruct((batch, seq, N_HEADS * V_HEAD), BF16),
        scratch_shapes=[pltpu.VMEM((2, TQ, LANES), F32), pltpu.VMEM((2, TQ, 2 * V_HEAD), F32)],
        compiler_params=pltpu.CompilerParams(dimension_semantics=("arbitrary",) * 3,
                                             vmem_limit_bytes=VMEM_LIMIT),
        name="mla_attention",
    )(q, k, v)


def _post_kernel(x_ref, y_ref, attn_ref, sgs_ref, sgm_ref, wglu_ref, bglu_ref, wos_ref, wom_ref, wout_ref,
                 nmlp_ref, wup_ref, wdown_ref, o_ref):
    z = jax.nn.gelu(y_ref[...])
    z = z * jax.nn.sigmoid(_dot(z.astype(BF16), wglu_ref[...]) + bglu_ref[...])
    y_ssm = _dot(z.astype(BF16), wos_ref[...])
    y_mla = _dot(attn_ref[...], wom_ref[...])
    mixed = sgs_ref[...].astype(F32) * y_ssm + sgm_ref[...].astype(F32) * y_mla
    h = x_ref[...] + _dot(mixed.astype(BF16), wout_ref[...])
    hn = _rms(h, nmlp_ref[...]).astype(BF16)
    acc = h
    for c in range(D_FF // FF_CHUNK):
        up = _dot(hn, wup_ref[:, c * FF_CHUNK:(c + 1) * FF_CHUNK])
        hid = jnp.square(jnp.maximum(up, 0.0)).astype(BF16)
        acc = acc + _dot(hid, wdown_ref[c * FF_CHUNK:(c + 1) * FF_CHUNK, :])
    o_ref[...] = acc


def _post(x2d, y2d, attn2d, sgs, sgm, prm):
    t = x2d.shape[0]
    tm = TM_POST
    row = lambda w: pl.BlockSpec((tm, w), lambda i: (i, 0))
    consts = [prm["w_glu"], prm["b_glu"], prm["w_o_ssm"], prm["w_o_mla"], prm["w_out"],
              prm["nmlp"], prm["w_up"], prm["w_down"]]
    return pl.pallas_call(
        _post_kernel,
        grid=(t // tm,),
        in_specs=[row(D_MODEL), row(SSM_WIDTH), row(D_MODEL), row(D_MODEL), row(D_MODEL)]
                 + [_const_spec(c.shape) for c in consts],
        out_specs=row(D_MODEL),
        out_shape=jax.ShapeDtypeStruct((t, D_MODEL), F32),
        compiler_params=pltpu.CompilerParams(dimension_semantics=("arbitrary",),
                                             vmem_limit_bytes=VMEM_LIMIT),
        name="merge_mlp",
    )(x2d, y2d, attn2d, sgs, sgm, *consts)


def _rot_cols(w):
    half = w.shape[-1] // 2
    return jnp.concatenate([-w[..., half:], w[..., :half]], axis=-1)


def _swap_halves(g):
    half = g.shape[-1] // 2
    return jnp.concatenate([g[..., half:], g[..., :half]], axis=-1)


def _head_indicators():
    eq = np.zeros((Q_MAIN, LANES), np.float32)
    etq = np.zeros((2 * LANES, Q_MAIN + Q_AUX), np.float32)
    for p in range(N_PAIRS):
        for hh in range(2):
            h = 2 * p + hh
            cols = list(range(p * PAIR_W + hh * QK_NOPE, p * PAIR_W + (hh + 1) * QK_NOPE))
            cols += list(range(p * PAIR_W + 2 * QK_NOPE + hh * QK_ROPE, p * PAIR_W + 2 * QK_NOPE + (hh + 1) * QK_ROPE))
            eq[cols, h] = 1.0
            aux = list(range(Q_MAIN + p * LANES + hh * QK_ROPE, Q_MAIN + p * LANES + (hh + 1) * QK_ROPE))
            for base in (0, LANES):
                etq[base + h, cols + aux] = 1.0
    nk = N_HEADS * QK_NOPE
    ek = np.zeros((nk + LANES, LANES), np.float32)
    etk = np.zeros((2 * LANES, nk + N_HEADS * LANES), np.float32)
    for h in range(N_HEADS):
        ek[h * QK_NOPE:(h + 1) * QK_NOPE, h] = 1.0
        ek[nk:, h] = 0.5
        half0 = nk + h * LANES + (h % 2) * QK_ROPE
        for base in (0, LANES):
            etk[base + h, h * QK_NOPE:(h + 1) * QK_NOPE] = 1.0
            etk[base + h, half0:half0 + QK_ROPE] = 1.0
    as_bf = lambda a: jnp.asarray(a, BF16)
    return as_bf(eq), as_bf(ek), as_bf(etq), as_bf(etk)


def _ssm_params(a_re, a_im, log_dt, b_re, b_im, c_re, c_im, d_skip):
    q = SSM_CHUNK
    dt = jnp.exp(log_dt.astype(F32))[:, None]
    lam = lax.complex(a_re.astype(F32), a_im.astype(F32))
    lam_bar = jnp.exp(lam * dt)
    b_bar = ((lam_bar - 1.0) / lam)[..., None] * lax.complex(b_re.astype(F32), b_im.astype(F32))
    c_mat = lax.complex(c_re.astype(F32), c_im.astype(F32))
    steps = jnp.arange(q + 1, dtype=F32)
    pw = jnp.exp((lam * dt)[:, None, :] * steps[None, :, None])
    cb = c_mat[:, None, :, None, :] * pw[:, :q, None, None, :] * jnp.swapaxes(b_bar, 1, 2)[:, None, None, :, :]
    kj = jnp.real(jnp.sum(cb, axis=-1))
    kj = kj.at[:, 0].add(jax.vmap(jnp.diag)(d_skip.astype(F32)))
    s_idx = np.arange(q)[:, None]
    t_idx = np.arange(q)[None, :]
    lag = np.clip(t_idx - s_idx, 0, q - 1)
    toe = kj[:, lag]
    toe = jnp.where((t_idx >= s_idx)[None, :, :, None, None], toe, 0.0)
    toe = jnp.transpose(toe, (0, 1, 4, 2, 3)).reshape(SSM_GROUPS, SSM_ROW, SSM_ROW)
    wst = pw[:, q - 1 - np.arange(q), :, None] * b_bar[:, None, :, :]
    wst = jnp.transpose(wst, (0, 1, 3, 2)).reshape(SSM_GROUPS, SSM_ROW, SSM_STATE)
    pad = jnp.zeros((SSM_GROUPS, SSM_ROW, LANES - SSM_STATE), F32)
    rhs = jnp.concatenate([toe, jnp.real(wst), pad, jnp.imag(wst), pad], axis=-1)
    cl = c_mat[:, None, :, :] * pw[:, 1:q + 1, None, :]
    cl = jnp.transpose(cl, (0, 3, 1, 2)).reshape(SSM_GROUPS, SSM_STATE, SSM_ROW)
    zrow = jnp.zeros((SSM_GROUPS, LANES - SSM_STATE, SSM_ROW), F32)
    wout = jnp.concatenate([jnp.real(cl), zrow, -jnp.imag(cl), zrow], axis=1)
    a_q = pw[:, q, :]
    lane_pad = jnp.zeros((SSM_GROUPS, LANES - SSM_STATE), F32)
    a_rows = jnp.stack([jnp.concatenate([jnp.real(a_q), lane_pad], -1),
                        jnp.concatenate([jnp.imag(a_q), lane_pad], -1)], axis=1)
    a_rows = jnp.concatenate([a_rows, jnp.zeros((SSM_GROUPS, 6, LANES), F32)], axis=1)
    return rhs.astype(BF16), a_rows, wout.astype(BF16)


def _prepare_params(norm_mix, w_in, q_a_norm, kv_a_norm, w_q_b, w_kv_b, q_norm, k_norm, w_o_mla,
                    ssm_a_re, ssm_a_im, ssm_log_dt, ssm_b_re, ssm_b_im, ssm_c_re, ssm_c_im, ssm_d,
                    w_glu, b_glu, w_o_ssm, w_out, norm_mlp, w_up, w_down):
    o_q = SSM_WIDTH
    o_kv = o_q + Q_LORA
    o_kp = o_kv + KV_LORA
    o_gs = o_kp + QK_ROPE
    w_kpe = w_in[:, o_kp:o_gs]
    w_kpr = _rot_cols(w_kpe)
    w_in2 = jnp.concatenate([w_in[:, :o_kp], w_kpe, w_kpe, w_kpr, w_kpr, w_in[:, o_gs:]], axis=1).astype(BF16)

    wq = w_q_b.reshape(Q_LORA, N_HEADS, QK_HEAD)
    main, aux, gqm, gqa = [], [], [], []
    qscale = (QK_HEAD ** -0.5) * math.log2(math.e)
    g_nope, g_rope = q_norm[:QK_NOPE] * qscale, q_norm[QK_NOPE:] * qscale
    for p in range(N_PAIRS):
        a, b = 2 * p, 2 * p + 1
        main += [wq[:, a, :QK_NOPE], wq[:, b, :QK_NOPE], wq[:, a, QK_NOPE:], wq[:, b, QK_NOPE:]]
        aux += [_rot_cols(wq[:, a, QK_NOPE:]), _rot_cols(wq[:, b, QK_NOPE:])]
        gqm += [g_nope, g_nope, g_rope, g_rope]
        gqa += [_swap_halves(g_rope)] * 2
    wq2 = jnp.concatenate(main + aux, axis=1).astype(BF16)

    wkv = w_kv_b.reshape(KV_LORA, N_HEADS, QK_NOPE + V_HEAD)
    wkv2 = jnp.concatenate([wkv[:, :, :QK_NOPE].reshape(KV_LORA, -1),
                            wkv[:, :, QK_NOPE:].reshape(KV_LORA, -1)], axis=1).astype(BF16)
    gk_rope = k_norm[QK_NOPE:]
    eq, ek, etq, etk = _head_indicators()
    rhs, a_rows, wout_ssm = _ssm_params(ssm_a_re, ssm_a_im, ssm_log_dt, ssm_b_re, ssm_b_im,
                                        ssm_c_re, ssm_c_im, ssm_d)
    row = lambda v: v.astype(F32).reshape(1, -1)
    return {
        "nmix": row(norm_mix), "w_in": w_in2, "qan": row(q_a_norm), "kvan": row(kv_a_norm),
        "wq": wq2, "wkv": wkv2,
        "gqm": row(jnp.concatenate(gqm)), "gqa": row(jnp.concatenate(gqa)),
        "gkn": row(jnp.tile(k_norm[:QK_NOPE], N_HEADS)),
        "gkx": row(jnp.tile(gk_rope, 2)), "gkr": row(jnp.tile(_swap_halves(gk_rope), 2)),
        "eq": eq, "ek": ek, "etq": etq, "etk": etk,
        "ssm_rhs": rhs, "ssm_a": a_rows, "ssm_wout": wout_ssm,
        "w_glu": w_glu.astype(BF16), "b_glu": row(b_glu), "w_o_ssm": w_o_ssm.astype(BF16),
        "w_o_mla": w_o_mla.astype(BF16), "w_out": w_out.astype(BF16), "nmlp": row(norm_mlp),
        "w_up": w_up.astype(BF16), "w_down": w_down.astype(BF16),
    }


def _layer(h, cos, sin, prm):
    batch, seq, _ = h.shape
    t = batch * seq
    n_chunks = seq // SSM_CHUNK
    x2d = h.reshape(t, D_MODEL)
    u, q, k, v, sgs, sgm = _in_proj(x2d, cos, sin, prm, batch, seq)
    u_g = u.reshape(batch, n_chunks, SSM_CHUNK, SSM_GROUPS, SSM_GROUP_CH)
    u_g = jnp.transpose(u_g, (3, 1, 0, 2, 4)).reshape(SSM_GROUPS, n_chunks * batch, SSM_ROW)
    y_g = _ssm(u_g, prm, batch, n_chunks)
    y = y_g.reshape(SSM_GROUPS, n_chunks, batch, SSM_CHUNK, SSM_GROUP_CH)
    y2d = jnp.transpose(y, (2, 1, 3, 0, 4)).reshape(t, SSM_WIDTH)
    attn = _attention(q.reshape(batch, seq, Q_MAIN), k, v.reshape(batch, seq, N_HEADS * V_HEAD), batch, seq)
    out = _post(x2d, y2d, attn.reshape(t, N_HEADS * V_HEAD), sgs, sgm, prm)
    return out.reshape(batch, seq, D_MODEL)


def kernel(x, positions, norm_mix, w_in, q_a_norm, kv_a_norm, w_q_b, w_kv_b, q_norm, k_norm, w_o_mla,
           ssm_a_re, ssm_a_im, ssm_log_dt, ssm_b_re, ssm_b_im, ssm_c_re, ssm_c_im, ssm_d,
           w_glu, b_glu, w_o_ssm, w_out, norm_mlp, w_up, w_down):
    params = (norm_mix, w_in, q_a_norm, kv_a_norm, w_q_b, w_kv_b, q_norm, k_norm, w_o_mla,
              ssm_a_re, ssm_a_im, ssm_log_dt, ssm_b_re, ssm_b_im, ssm_c_re, ssm_c_im, ssm_d,
              w_glu, b_glu, w_o_ssm, w_out, norm_mlp, w_up, w_down)
    cos, sin = _rope_tables(positions)
    h = x
    for layer in range(norm_mix.shape[0]):
        h = _layer(h, cos, sin, _prepare_params(*[p[layer] for p in params]))
    return h
```

```python
import functools
import math

import numpy as np
import jax
import jax.numpy as jnp
from jax import lax
from jax.experimental import pallas as pl
from jax.experimental.pallas import tpu as pltpu

D_MODEL = 1024
SSM_GROUPS = 32
SSM_GROUP_CH = 16
SSM_WIDTH = SSM_GROUPS * SSM_GROUP_CH
SSM_STATE = 64
N_HEADS = 8
N_PAIRS = N_HEADS // 2
QK_NOPE = 128
QK_ROPE = 64
QK_HEAD = QK_NOPE + QK_ROPE
V_HEAD = 128
Q_LORA = 384
KV_LORA = 256
ROPE_THETA = 10000.0
D_FF = 4 * D_MODEL
EPS = 1e-6

LANES = 128
SUBLANES = 8
SSM_CHUNK = SUBLANES
GROUPS_PER_TILE = LANES // SSM_GROUP_CH
SSM_TILES = SSM_WIDTH // LANES
SSM_K = SSM_CHUNK * LANES
SSM_ST = GROUPS_PER_TILE * SSM_STATE
SSM_ROWS = 1024
HEAD_PAD = 256
PAIR_W = 2 * QK_NOPE + LANES
Q_MAIN = N_PAIRS * PAIR_W
Q_AUX = N_PAIRS * LANES
VMEM_LIMIT = 56 * 1024 * 1024

TM_PROJ = 512
TM_POST = 512
TQ = 512
TK = 512
FF_CHUNK = 1024

F32 = jnp.float32
BF16 = jnp.bfloat16


def _dot(a, b):
    return jnp.dot(a, b, preferred_element_type=F32)


def _rms(x, gain):
    ms = jnp.mean(x * x, axis=-1, keepdims=True)
    return x * lax.rsqrt(ms + EPS) * gain


def _rope_kernel(pos_ref, freq_ref, cos_ref, sin_ref):
    ang = freq_ref[...] * pos_ref[0].astype(F32)
    cos_ref[0] = jnp.cos(ang)
    sin_ref[0] = jnp.sin(ang)


def _rope_tables(positions):
    b, l = positions.shape
    half = QK_ROPE // 2
    inv_freq = ROPE_THETA ** (-jnp.arange(half, dtype=F32) / half)
    out = jax.ShapeDtypeStruct((b, half, l), F32)
    cos_t, sin_t = pl.pallas_call(
        _rope_kernel,
        grid=(b,),
        in_specs=[pl.BlockSpec((1, 1, l), lambda i: (i, 0, 0)),
                  pl.BlockSpec((half, 1), lambda i: (0, 0))],
        out_specs=[pl.BlockSpec((1, half, l), lambda i: (i, 0, 0))] * 2,
        out_shape=[out, out],
        name="rope_tables",
    )(positions.reshape(b, 1, l), inv_freq.reshape(half, 1))

    def expand(t):
        t = jnp.transpose(t, (0, 2, 1))
        return jnp.tile(t, (1, 1, LANES // half)).reshape(b * l, LANES)

    return expand(cos_t), expand(sin_t)


def _in_proj_kernel(x_ref, cos_ref, sin_ref, nmix_ref, w_in_ref, qan_ref, kvan_ref, wq_ref, wkv_ref,
                    gqm_ref, gqa_ref, gkn_ref, gkx_ref, gkr_ref, eq_ref, ek_ref, etq_ref, etk_ref,
                    u_ref, q_ref, k_ref, v_ref, sgs_ref, sgm_ref):
    o_q = SSM_WIDTH
    o_kv = o_q + Q_LORA
    o_kp = o_kv + KV_LORA
    o_kr = o_kp + LANES
    o_gs = o_kr + LANES
    o_gm = o_gs + D_MODEL

    xn = _rms(x_ref[...], nmix_ref[...]).astype(BF16)
    cos = cos_ref[...]
    sin = sin_ref[...]

    u_ref[...] = _dot(xn, w_in_ref[:, 0:o_q]).reshape(u_ref.shape)
    sgs_ref[...] = jax.nn.sigmoid(_dot(xn, w_in_ref[:, o_gs:o_gm])).astype(BF16)
    sgm_ref[...] = jax.nn.sigmoid(_dot(xn, w_in_ref[:, o_gm:o_gm + D_MODEL])).astype(BF16)

    def head_scales(sq, e_ref, et_ref):
        sums = _dot(sq.astype(BF16), e_ref[...])
        inv = lax.rsqrt(sums * (1.0 / QK_HEAD) + EPS)
        hi = inv.astype(BF16)
        lo = (inv - hi.astype(F32)).astype(BF16)
        return _dot(jnp.concatenate([hi, lo], axis=1), et_ref[...])

    qn = _rms(_dot(xn, w_in_ref[:, o_q:o_kv]), qan_ref[...]).astype(BF16)
    qraw = _dot(qn, wq_ref[...])
    qmain = qraw[:, 0:Q_MAIN]
    qs = qraw * head_scales(qmain * qmain, eq_ref, etq_ref)
    for p in range(N_PAIRS):
        lo_, mid, hi_ = p * PAIR_W, p * PAIR_W + 2 * QK_NOPE, (p + 1) * PAIR_W
        q_ref[:, lo_:mid] = (qs[:, lo_:mid] * gqm_ref[:, lo_:mid]).astype(BF16)
        xr_lo = Q_MAIN + p * LANES
        rope = (qs[:, mid:hi_] * gqm_ref[:, mid:hi_]) * cos \
            + (qs[:, xr_lo:xr_lo + LANES] * gqa_ref[:, p * LANES:(p + 1) * LANES]) * sin
        q_ref[:, mid:hi_] = rope.astype(BF16)

    ckv = _rms(_dot(xn, w_in_ref[:, o_kv:o_kp]), kvan_ref[...]).astype(BF16)
    kvraw = _dot(ckv, wkv_ref[...])
    knope = kvraw[:, 0:N_HEADS * QK_NOPE]
    v_ref[...] = kvraw[:, N_HEADS * QK_NOPE:].astype(BF16)
    kpe = _dot(xn, w_in_ref[:, o_kp:o_kr])
    kpe_rot = _dot(xn, w_in_ref[:, o_kr:o_gs])
    ksc = head_scales(jnp.concatenate([knope * knope, kpe * kpe], axis=1), ek_ref, etk_ref)
    krope = (kpe * gkx_ref[...]) * cos + (kpe_rot * gkr_ref[...]) * sin
    kn = knope * ksc[:, 0:N_HEADS * QK_NOPE] * gkn_ref[...]
    for h in range(N_HEADS):
        k_ref[0, h, :, 0:QK_NOPE] = kn[:, h * QK_NOPE:(h + 1) * QK_NOPE].astype(BF16)
        off = N_HEADS * QK_NOPE + h * LANES
        k_ref[0, h, :, QK_NOPE:HEAD_PAD] = (krope * ksc[:, off:off + LANES]).astype(BF16)


def _const_spec(shape):
    nd = len(shape)
    return pl.BlockSpec(shape, lambda *_: (0,) * nd, pipeline_mode=pl.Buffered(1))


def _chunk_rows_spec(tm, nlt):
    return pl.BlockSpec((tm // SSM_CHUNK, None, SSM_CHUNK, SSM_WIDTH), lambda i: (i % nlt, i // nlt, 0, 0))


def _in_proj(x2d, cos, sin, prm, batch, seq):
    t = x2d.shape[0]
    tm = TM_PROJ
    nlt = seq // tm
    row = lambda w: pl.BlockSpec((tm, w), lambda i: (i, 0))
    consts = [prm["nmix"], prm["w_in"], prm["qan"], prm["kvan"], prm["wq"], prm["wkv"],
              prm["gqm"], prm["gqa"], prm["gkn"], prm["gkx"], prm["gkr"],
              prm["eq"], prm["ek"], prm["etq"], prm["etk"]]
    out_shape = [
        jax.ShapeDtypeStruct((seq // SSM_CHUNK, batch, SSM_CHUNK, SSM_WIDTH), F32),
        jax.ShapeDtypeStruct((t, Q_MAIN), BF16),
        jax.ShapeDtypeStruct((batch, N_HEADS, seq, HEAD_PAD), BF16),
        jax.ShapeDtypeStruct((t, N_HEADS * V_HEAD), BF16),
        jax.ShapeDtypeStruct((t, D_MODEL), BF16),
        jax.ShapeDtypeStruct((t, D_MODEL), BF16),
    ]
    out_specs = [
        _chunk_rows_spec(tm, nlt), row(Q_MAIN),
        pl.BlockSpec((1, N_HEADS, tm, HEAD_PAD), lambda i: (i // nlt, 0, i % nlt, 0)),
        row(N_HEADS * V_HEAD), row(D_MODEL), row(D_MODEL),
    ]
    return pl.pallas_call(
        _in_proj_kernel,
        grid=(t // tm,),
        in_specs=[row(D_MODEL), row(LANES), row(LANES)] + [_const_spec(c.shape) for c in consts],
        out_specs=out_specs,
        out_shape=out_shape,
        compiler_params=pltpu.CompilerParams(dimension_semantics=("arbitrary",),
                                             vmem_limit_bytes=VMEM_LIMIT),
        name="in_proj",
    )(x2d, cos, sin, *consts)


def _ssm_kernel(*refs):
    nq = SSM_CHUNK
    u_refs = refs[0:nq]
    wtoe_ref, wst_ref, wout_ref, a_ref, y_ref, s_sc, h_sc, carry_sc, y_sc = refs[nq:]
    rows = s_sc.shape[0]
    n_chunks = rows // SUBLANES

    @pl.when(pl.program_id(1) == 0)
    def _():
        carry_sc[...] = jnp.zeros(carry_sc.shape, F32)

    lhs = jnp.concatenate([u[...] for u in u_refs], axis=1).astype(BF16)
    s_sc[...] = _dot(lhs, wst_ref[0])
    a_re = jnp.broadcast_to(a_ref[0, 0:1, :], (SUBLANES, SSM_ST))
    a_im = jnp.broadcast_to(a_ref[0, 1:2, :], (SUBLANES, SSM_ST))

    def body(k, carry):
        h_re, h_im = carry
        r0 = pl.multiple_of(k * SUBLANES, SUBLANES)
        h_sc[pl.ds(r0, SUBLANES), 0:SSM_ST] = h_re
        h_sc[pl.ds(r0, SUBLANES), SSM_ST:2 * SSM_ST] = h_im
        s_re = s_sc[pl.ds(r0, SUBLANES), 0:SSM_ST]
        s_im = s_sc[pl.ds(r0, SUBLANES), SSM_ST:2 * SSM_ST]
        return (a_re * h_re - a_im * h_im + s_re, a_re * h_im + a_im * h_re + s_im)

    h_re, h_im = lax.fori_loop(0, rows // SUBLANES, body,
                               (carry_sc[:, 0:SSM_ST], carry_sc[:, SSM_ST:2 * SSM_ST]), unroll=4)
    carry_sc[:, 0:SSM_ST] = h_re
    carry_sc[:, SSM_ST:2 * SSM_ST] = h_im

    hb = h_sc[...].astype(BF16)
    for tt in range(nq // 2):
        kk = (2 * tt + 2) * LANES
        c0, c1 = 2 * tt * LANES, (2 * tt + 2) * LANES
        y = _dot(lhs[:, 0:kk], wtoe_ref[0, 0:kk, c0:c1]) + _dot(hb, wout_ref[0, :, c0:c1])
        y_sc[0] = y[:, 0:LANES]
        y_sc[1] = y[:, LANES:2 * LANES]
        for j in range(2):
            for b in range(y_ref.shape[0]):
                y_ref[b, pl.ds(2 * tt + j, n_chunks, stride=SSM_CHUNK), :] = \
                    y_sc[j, pl.ds(b, n_chunks, stride=y_ref.shape[0]), :]


def _ssm(u_rows, prm, batch, seq):
    n_rows = u_rows.shape[0]
    rows = SSM_ROWS
    assert batch == SUBLANES, "one scan step works on a full vreg of batch rows"
    col = lambda s: pl.BlockSpec((rows, LANES), lambda g, r, s=s: (r, s * SSM_TILES + g))
    wspec = lambda: pl.BlockSpec((1, SSM_K, SSM_K), lambda g, r: (g, 0, 0))
    return pl.pallas_call(
        _ssm_kernel,
        grid=(SSM_TILES, n_rows // rows),
        in_specs=[col(s) for s in range(SSM_CHUNK)] + [wspec(), wspec(), wspec(),
                  pl.BlockSpec((1, 2, SSM_ST), lambda g, r: (g, 0, 0))],
        out_specs=pl.BlockSpec((batch, rows, LANES), lambda g, r: (0, r, g)),
        out_shape=jax.ShapeDtypeStruct((batch, seq, SSM_WIDTH), F32),
        scratch_shapes=[pltpu.VMEM((rows, 2 * SSM_ST), F32), pltpu.VMEM((rows, 2 * SSM_ST), F32),
                        pltpu.VMEM((SUBLANES, 2 * SSM_ST), F32), pltpu.VMEM((2, rows, LANES), F32)],
        compiler_params=pltpu.CompilerParams(dimension_semantics=("arbitrary", "arbitrary"),
                                             vmem_limit_bytes=VMEM_LIMIT),
        name="ssm_scan",
    )(*([u_rows] * SSM_CHUNK), prm["ssm_wtoe"], prm["ssm_wst"], prm["ssm_wout"], prm["ssm_a"])


def _attn_kernel(q_ref, k_ref, v_ref, o_ref, m_sc, acc_sc):
    i = pl.program_id(2)
    ones = jnp.ones((TK, LANES), BF16)
    nt = (((1,), (1,)), ((), ()))

    def block(hh, q, kv0, masked):
        k = k_ref[0, hh, pl.ds(kv0, TK), :]
        s = lax.dot_general(q, k, nt, preferred_element_type=F32)
        if masked:
            r = lax.broadcasted_iota(jnp.int32, (TQ, TK), 0)
            c = lax.broadcasted_iota(jnp.int32, (TQ, TK), 1)
            s = jnp.where(c <= r, s, -jnp.inf)
        m_prev = m_sc[hh]
        m_new = jnp.maximum(m_prev, jnp.max(s, axis=-1, keepdims=True))
        alpha = jnp.exp2(m_prev - m_new)
        p = jnp.exp2(s - jnp.tile(m_new, (1, TK // LANES))).astype(BF16)
        v_ext = jnp.concatenate([v_ref[0, pl.ds(kv0, TK), hh * V_HEAD:(hh + 1) * V_HEAD], ones], axis=1)
        acc_sc[hh] = acc_sc[hh] * jnp.tile(alpha, (1, 2)) + _dot(p, v_ext)
        m_sc[hh] = m_new

    m_sc[...] = jnp.full(m_sc.shape, -jnp.inf, F32)
    acc_sc[...] = jnp.zeros(acc_sc.shape, F32)
    qs = [jnp.concatenate([q_ref[0, :, hh * QK_NOPE:(hh + 1) * QK_NOPE], q_ref[0, :, 2 * QK_NOPE:PAIR_W]], axis=1)
          for hh in range(2)]

    def body(j, carry):
        kv0 = pl.multiple_of(j * TK, TK)
        for hh in range(2):
            block(hh, qs[hh], kv0, False)
        return carry

    lax.fori_loop(0, i, body, 0)
    kv_diag = pl.multiple_of(i * TK, TK)
    for hh in range(2):
        block(hh, qs[hh], kv_diag, True)
        acc = acc_sc[hh]
        o_ref[0, :, hh * V_HEAD:(hh + 1) * V_HEAD] = (acc[:, 0:V_HEAD] / acc[:, V_HEAD:2 * V_HEAD]).astype(BF16)


def _attention(q, k, v, batch, seq):
    assert TQ == TK
    return pl.pallas_call(
        _attn_kernel,
        grid=(batch, N_PAIRS, seq // TQ),
        in_specs=[pl.BlockSpec((1, TQ, PAIR_W), lambda b, p, i: (b, i, p)),
                  pl.BlockSpec((1, 2, seq, HEAD_PAD), lambda b, p, i: (b, p, 0, 0)),
                  pl.BlockSpec((1, seq, 2 * V_HEAD), lambda b, p, i: (b, 0, p))],
        out_specs=pl.BlockSpec((1, TQ, 2 * V_HEAD), lambda b, p, i: (b, i, p)),
        out_shape=jax.ShapeDtypeStruct((batch, seq, N_HEADS * V_HEAD), BF16),
        scratch_shapes=[pltpu.VMEM((2, TQ, LANES), F32), pltpu.VMEM((2, TQ, 2 * V_HEAD), F32)],
        compiler_params=pltpu.CompilerParams(dimension_semantics=("arbitrary",) * 3,
                                             vmem_limit_bytes=VMEM_LIMIT),
        name="mla_attention",
    )(q, k, v)


def _post_kernel(x_ref, y_ref, attn_ref, sgs_ref, sgm_ref, wglu_ref, bglu_ref, wos_ref, wom_ref, wout_ref,
                 nmlp_ref, wup_ref, wdown_ref, o_ref):
    z = jax.nn.gelu(y_ref[...])
    z = z * jax.nn.sigmoid(_dot(z.astype(BF16), wglu_ref[...]) + bglu_ref[...])
    y_ssm = _dot(z.astype(BF16), wos_ref[...])
    y_mla = _dot(attn_ref[...], wom_ref[...])
    mixed = sgs_ref[...].astype(F32) * y_ssm + sgm_ref[...].astype(F32) * y_mla
    h = x_ref[...] + _dot(mixed.astype(BF16), wout_ref[...])
    hn = _rms(h, nmlp_ref[...]).astype(BF16)
    acc = h
    for c in range(D_FF // FF_CHUNK):
        up = _dot(hn, wup_ref[:, c * FF_CHUNK:(c + 1) * FF_CHUNK])
        hid = jnp.square(jnp.maximum(up, 0.0)).astype(BF16)
        acc = acc + _dot(hid, wdown_ref[c * FF_CHUNK:(c + 1) * FF_CHUNK, :])
    o_ref[...] = acc


def _post(x2d, y2d, attn2d, sgs, sgm, prm):
    t = x2d.shape[0]
    tm = TM_POST
    row = lambda w: pl.BlockSpec((tm, w), lambda i: (i, 0))
    consts = [prm["w_glu"], prm["b_glu"], prm["w_o_ssm"], prm["w_o_mla"], prm["w_out"],
              prm["nmlp"], prm["w_up"], prm["w_down"]]
    return pl.pallas_call(
        _post_kernel,
        grid=(t // tm,),
        in_specs=[row(D_MODEL), row(SSM_WIDTH), row(D_MODEL), row(D_MODEL), row(D_MODEL)]
                 + [_const_spec(c.shape) for c in consts],
        out_specs=row(D_MODEL),
        out_shape=jax.ShapeDtypeStruct((t, D_MODEL), F32),
        compiler_params=pltpu.CompilerParams(dimension_semantics=("arbitrary",),
                                             vmem_limit_bytes=VMEM_LIMIT),
        name="merge_mlp",
    )(x2d, y2d, attn2d, sgs, sgm, *consts)


def _rot_cols(w):
    half = w.shape[-1] // 2
    return jnp.concatenate([-w[..., half:], w[..., :half]], axis=-1)


def _swap_halves(g):
    half = g.shape[-1] // 2
    return jnp.concatenate([g[..., half:], g[..., :half]], axis=-1)


def _head_indicators():
    eq = np.zeros((Q_MAIN, LANES), np.float32)
    etq = np.zeros((2 * LANES, Q_MAIN + Q_AUX), np.float32)
    for p in range(N_PAIRS):
        for hh in range(2):
            h = 2 * p + hh
            cols = list(range(p * PAIR_W + hh * QK_NOPE, p * PAIR_W + (hh + 1) * QK_NOPE))
            cols += list(range(p * PAIR_W + 2 * QK_NOPE + hh * QK_ROPE, p * PAIR_W + 2 * QK_NOPE + (hh + 1) * QK_ROPE))
            eq[cols, h] = 1.0
            aux = list(range(Q_MAIN + p * LANES + hh * QK_ROPE, Q_MAIN + p * LANES + (hh + 1) * QK_ROPE))
            for base in (0, LANES):
                etq[base + h, cols + aux] = 1.0
    nk = N_HEADS * QK_NOPE
    ek = np.zeros((nk + LANES, LANES), np.float32)
    etk = np.zeros((2 * LANES, nk + N_HEADS * LANES), np.float32)
    for h in range(N_HEADS):
        ek[h * QK_NOPE:(h + 1) * QK_NOPE, h] = 1.0
        ek[nk:, h] = 0.5
        half0 = nk + h * LANES + (h % 2) * QK_ROPE
        for base in (0, LANES):
            etk[base + h, h * QK_NOPE:(h + 1) * QK_NOPE] = 1.0
            etk[base + h, half0:half0 + QK_ROPE] = 1.0
    as_bf = lambda a: jnp.asarray(a, BF16)
    return as_bf(eq), as_bf(ek), as_bf(etq), as_bf(etk)


def _block_diag_expand(compact, row_div, col_div, rep):
    r = (np.arange(SSM_K) // row_div) % GROUPS_PER_TILE
    c = (np.arange(SSM_K) // col_div) % GROUPS_PER_TILE
    mask = jnp.asarray(r[:, None] == c[None, :], F32)
    wide = jnp.einsum("grk,kc->grc", compact, jnp.asarray(rep, F32))
    return (wide * mask).astype(BF16)


def _ssm_params(a_re, a_im, log_dt, b_re, b_im, c_re, c_im, d_skip):
    q, g8, nt = SSM_CHUNK, GROUPS_PER_TILE, SSM_TILES
    hp = lax.Precision.HIGHEST
    ar, ai = a_re.astype(F32), a_im.astype(F32)
    dt = jnp.exp(log_dt.astype(F32))[:, None]
    steps = jnp.arange(q + 1, dtype=F32)[None, :, None]
    mag = jnp.exp((ar * dt)[:, None, :] * steps)
    ang = (ai * dt)[:, None, :] * steps
    pr, pi = mag * jnp.cos(ang), mag * jnp.sin(ang)
    x, y = pr[:, 1] - 1.0, pi[:, 1]
    den = ar * ar + ai * ai
    cr, ci = (x * ar + y * ai) / den, (y * ar - x * ai) / den
    br, bi = b_re.astype(F32), b_im.astype(F32)
    bbr = cr[..., None] * br - ci[..., None] * bi
    bbi = cr[..., None] * bi + ci[..., None] * br
    cre, cim = c_re.astype(F32), c_im.astype(F32)
    cpr = cre[:, None] * pr[:, :, None, :] - cim[:, None] * pi[:, :, None, :]
    cpi = cre[:, None] * pi[:, :, None, :] + cim[:, None] * pr[:, :, None, :]
    kj = (jnp.einsum("gjcp,gpi->gjci", cpr[:, :q], bbr, precision=hp)
          - jnp.einsum("gjcp,gpi->gjci", cpi[:, :q], bbi, precision=hp))
    kj = kj.at[:, 0].add(jax.vmap(jnp.diag)(d_skip.astype(F32)))
    s_idx = np.arange(q)[:, None]
    t_idx = np.arange(q)[None, :]
    toe = kj[:, np.clip(t_idx - s_idx, 0, q - 1)]
    toe = jnp.where((t_idx >= s_idx)[None, :, :, None, None], toe, 0.0)
    toe = jnp.transpose(toe, (0, 1, 4, 2, 3))
    rev = q - 1 - np.arange(q)
    pr_s, pi_s = pr[:, rev][..., None], pi[:, rev][..., None]
    st_re = jnp.swapaxes(pr_s * bbr[:, None] - pi_s * bbi[:, None], 2, 3)
    st_im = jnp.swapaxes(pr_s * bbi[:, None] + pi_s * bbr[:, None], 2, 3)
    wst = jnp.stack([st_re, st_im], axis=3)
    cl = jnp.stack([cpr[:, 1:], -cpi[:, 1:]], axis=1)
    cl = jnp.transpose(cl, (0, 1, 4, 2, 3))

    def tile_major(w, lead):
        w = w.reshape((nt, g8) + lead + (LANES,))
        return jnp.swapaxes(w, 1, 2).reshape(nt, SSM_K, LANES)

    toe_c = tile_major(toe.reshape(SSM_GROUPS, q, SSM_GROUP_CH, LANES), (q, SSM_GROUP_CH))
    wst_c = tile_major(wst.reshape(SSM_GROUPS, q, SSM_GROUP_CH, LANES), (q, SSM_GROUP_CH))
    out_c = tile_major(cl.reshape(SSM_GROUPS, 2, SSM_STATE, LANES), (2, SSM_STATE))
    cols = np.arange(SSM_K)
    rep_tc = (np.arange(LANES)[:, None] == ((cols // LANES) * SSM_GROUP_CH + cols % SSM_GROUP_CH)[None, :])
    rep_rp = (np.arange(LANES)[:, None] == ((cols // SSM_ST) * SSM_STATE + cols % SSM_STATE)[None, :])
    wtoe = _block_diag_expand(toe_c, SSM_GROUP_CH, SSM_GROUP_CH, rep_tc)
    wst = _block_diag_expand(wst_c, SSM_GROUP_CH, SSM_STATE, rep_rp)
    wout = _block_diag_expand(out_c, SSM_STATE, SSM_GROUP_CH, rep_tc)
    a_rows = jnp.stack([pr[:, q].reshape(nt, SSM_ST), pi[:, q].reshape(nt, SSM_ST)], axis=1)
    return wtoe, wst, wout, a_rows


def _prepare_params(norm_mix, w_in, q_a_norm, kv_a_norm, w_q_b, w_kv_b, q_norm, k_norm, w_o_mla,
                    ssm_a_re, ssm_a_im, ssm_log_dt, ssm_b_re, ssm_b_im, ssm_c_re, ssm_c_im, ssm_d,
                    w_glu, b_glu, w_o_ssm, w_out, norm_mlp, w_up, w_down):
    o_q = SSM_WIDTH
    o_kv = o_q + Q_LORA
    o_kp = o_kv + KV_LORA
    o_gs = o_kp + QK_ROPE
    w_kpe = w_in[:, o_kp:o_gs]
    w_kpr = _rot_cols(w_kpe)
    w_in2 = jnp.concatenate([w_in[:, :o_kp], w_kpe, w_kpe, w_kpr, w_kpr, w_in[:, o_gs:]], axis=1).astype(BF16)

    wq = w_q_b.reshape(Q_LORA, N_HEADS, QK_HEAD)
    main, aux, gqm, gqa = [], [], [], []
    qscale = (QK_HEAD ** -0.5) * math.log2(math.e)
    g_nope, g_rope = q_norm[:QK_NOPE] * qscale, q_norm[QK_NOPE:] * qscale
    for p in range(N_PAIRS):
        a, b = 2 * p, 2 * p + 1
        main += [wq[:, a, :QK_NOPE], wq[:, b, :QK_NOPE], wq[:, a, QK_NOPE:], wq[:, b, QK_NOPE:]]
        aux += [_rot_cols(wq[:, a, QK_NOPE:]), _rot_cols(wq[:, b, QK_NOPE:])]
        gqm += [g_nope, g_nope, g_rope, g_rope]
        gqa += [_swap_halves(g_rope)] * 2
    wq2 = jnp.concatenate(main + aux, axis=1).astype(BF16)

    wkv = w_kv_b.reshape(KV_LORA, N_HEADS, QK_NOPE + V_HEAD)
    wkv2 = jnp.concatenate([wkv[:, :, :QK_NOPE].reshape(KV_LORA, -1),
                            wkv[:, :, QK_NOPE:].reshape(KV_LORA, -1)], axis=1).astype(BF16)
    gk_rope = k_norm[QK_NOPE:]
    eq, ek, etq, etk = _head_indicators()
    wtoe, wst, wout_ssm, a_rows = _ssm_params(ssm_a_re, ssm_a_im, ssm_log_dt, ssm_b_re, ssm_b_im,
                                              ssm_c_re, ssm_c_im, ssm_d)
    row = lambda v: v.astype(F32).reshape(1, -1)
    return {
        "nmix": row(norm_mix), "w_in": w_in2, "qan": row(q_a_norm), "kvan": row(kv_a_norm),
        "wq": wq2, "wkv": wkv2,
        "gqm": row(jnp.concatenate(gqm)), "gqa": row(jnp.concatenate(gqa)),
        "gkn": row(jnp.tile(k_norm[:QK_NOPE], N_HEADS)),
        "gkx": row(jnp.tile(gk_rope, 2)), "gkr": row(jnp.tile(_swap_halves(gk_rope), 2)),
        "eq": eq, "ek": ek, "etq": etq, "etk": etk,
        "ssm_wtoe": wtoe, "ssm_wst": wst, "ssm_wout": wout_ssm, "ssm_a": a_rows,
        "w_glu": w_glu.astype(BF16), "b_glu": row(b_glu), "w_o_ssm": w_o_ssm.astype(BF16),
        "w_o_mla": w_o_mla.astype(BF16), "w_out": w_out.astype(BF16), "nmlp": row(norm_mlp),
        "w_up": w_up.astype(BF16), "w_down": w_down.astype(BF16),
    }


def _layer(h, cos, sin, prm):
    batch, seq, _ = h.shape
    t = batch * seq
    x2d = h.reshape(t, D_MODEL)
    u, q, k, v, sgs, sgm = _in_proj(x2d, cos, sin, prm, batch, seq)
    n_rows = (seq // SSM_CHUNK) * batch
    y = _ssm(u.reshape(n_rows, SSM_CHUNK * SSM_WIDTH), prm, batch, seq)
    attn = _attention(q.reshape(batch, seq, Q_MAIN), k, v.reshape(batch, seq, N_HEADS * V_HEAD), batch, seq)
    out = _post(x2d, y.reshape(t, SSM_WIDTH), attn.reshape(t, N_HEADS * V_HEAD), sgs, sgm, prm)
    return out.reshape(batch, seq, D_MODEL)


def kernel(x, positions, norm_mix, w_in, q_a_norm, kv_a_norm, w_q_b, w_kv_b, q_norm, k_norm, w_o_mla,
           ssm_a_re, ssm_a_im, ssm_log_dt, ssm_b_re, ssm_b_im, ssm_c_re, ssm_c_im, ssm_d,
           w_glu, b_glu, w_o_ssm, w_out, norm_mlp, w_up, w_down):
    params = (norm_mix, w_in, q_a_norm, kv_a_norm, w_q_b, w_kv_b, q_norm, k_norm, w_o_mla,
              ssm_a_re, ssm_a_im, ssm_log_dt, ssm_b_re, ssm_b_im, ssm_c_re, ssm_c_im, ssm_d,
              w_glu, b_glu, w_o_ssm, w_out, norm_mlp, w_up, w_down)
    cos, sin = _rope_tables(positions)
    h = x
    for layer in range(norm_mix.shape[0]):
        h = _layer(h, cos, sin, _prepare_params(*[p[layer] for p in params]))
    return h
```

```python
import functools
import math

import numpy as np
import jax
import jax.numpy as jnp
from jax import lax
from jax.experimental import pallas as pl
from jax.experimental.pallas import tpu as pltpu

D_MODEL = 1024
SSM_GROUPS = 32
SSM_GROUP_CH = 16
SSM_WIDTH = SSM_GROUPS * SSM_GROUP_CH
SSM_STATE = 64
N_HEADS = 8
N_PAIRS = N_HEADS // 2
QK_NOPE = 128
QK_ROPE = 64
QK_HEAD = QK_NOPE + QK_ROPE
V_HEAD = 128
Q_LORA = 384
KV_LORA = 256
ROPE_THETA = 10000.0
D_FF = 4 * D_MODEL
EPS = 1e-6

LANES = 128
SUBLANES = 8
SSM_CHUNK = SUBLANES
GROUPS_PER_TILE = LANES // SSM_GROUP_CH
SSM_TILES = SSM_WIDTH // LANES
SSM_K = SSM_CHUNK * LANES
SSM_ST = GROUPS_PER_TILE * SSM_STATE
SSM_ROWS = 1024
HEAD_PAD = 256
PAIR_W = 2 * QK_NOPE + LANES
Q_MAIN = N_PAIRS * PAIR_W
Q_AUX = N_PAIRS * LANES
VMEM_LIMIT = 56 * 1024 * 1024

TM_PROJ = 512
TM_POST = 512
TQ = 512
TK = 512
ATTN_ROWS = 64
ATTN_HEADS = 4
FF_CHUNK = 1024

F32 = jnp.float32
BF16 = jnp.bfloat16


def _dot(a, b):
    return jnp.dot(a, b, preferred_element_type=F32)


def _rms(x, gain):
    ms = jnp.mean(x * x, axis=-1, keepdims=True)
    return x * lax.rsqrt(ms + EPS) * gain


def _rope_kernel(pos_ref, freq_ref, cos_ref, sin_ref):
    ang = freq_ref[...] * pos_ref[0].astype(F32)
    cos_ref[0] = jnp.cos(ang)
    sin_ref[0] = jnp.sin(ang)


def _rope_tables(positions):
    b, l = positions.shape
    half = QK_ROPE // 2
    inv_freq = ROPE_THETA ** (-jnp.arange(half, dtype=F32) / half)
    out = jax.ShapeDtypeStruct((b, half, l), F32)
    cos_t, sin_t = pl.pallas_call(
        _rope_kernel,
        grid=(b,),
        in_specs=[pl.BlockSpec((1, 1, l), lambda i: (i, 0, 0)),
                  pl.BlockSpec((half, 1), lambda i: (0, 0))],
        out_specs=[pl.BlockSpec((1, half, l), lambda i: (i, 0, 0))] * 2,
        out_shape=[out, out],
        name="rope_tables",
    )(positions.reshape(b, 1, l), inv_freq.reshape(half, 1))

    def expand(t):
        t = jnp.transpose(t, (0, 2, 1))
        return jnp.tile(t, (1, 1, LANES // half)).reshape(b * l, LANES)

    return expand(cos_t), expand(sin_t)


def _in_proj_kernel(x_ref, cos_ref, sin_ref, nmix_ref, w_in_ref, qan_ref, kvan_ref, wq_ref, wkv_ref,
                    gqm_ref, gqa_ref, gkn_ref, gkx_ref, gkr_ref, eq_ref, ek_ref, etq_ref, etk_ref,
                    u_ref, q_ref, k_ref, v_ref, sgs_ref, sgm_ref):
    o_q = SSM_WIDTH
    o_kv = o_q + Q_LORA
    o_kp = o_kv + KV_LORA
    o_kr = o_kp + LANES
    o_gs = o_kr + LANES
    o_gm = o_gs + D_MODEL

    xn = _rms(x_ref[...], nmix_ref[...]).astype(BF16)
    cos = cos_ref[...]
    sin = sin_ref[...]

    u_ref[...] = _dot(xn, w_in_ref[:, 0:o_q]).reshape(u_ref.shape)
    sgs_ref[...] = jax.nn.sigmoid(_dot(xn, w_in_ref[:, o_gs:o_gm])).astype(BF16)
    sgm_ref[...] = jax.nn.sigmoid(_dot(xn, w_in_ref[:, o_gm:o_gm + D_MODEL])).astype(BF16)

    def head_scales(sq, e_ref, et_ref):
        sums = _dot(sq.astype(BF16), e_ref[...])
        inv = lax.rsqrt(sums * (1.0 / QK_HEAD) + EPS)
        hi = inv.astype(BF16)
        lo = (inv - hi.astype(F32)).astype(BF16)
        return _dot(jnp.concatenate([hi, lo], axis=1), et_ref[...])

    qn = _rms(_dot(xn, w_in_ref[:, o_q:o_kv]), qan_ref[...]).astype(BF16)
    qraw = _dot(qn, wq_ref[...])
    qmain = qraw[:, 0:Q_MAIN]
    qs = qraw * head_scales(qmain * qmain, eq_ref, etq_ref)
    for p in range(N_PAIRS):
        lo_, mid, hi_ = p * PAIR_W, p * PAIR_W + 2 * QK_NOPE, (p + 1) * PAIR_W
        q_ref[:, lo_:mid] = (qs[:, lo_:mid] * gqm_ref[:, lo_:mid]).astype(BF16)
        xr_lo = Q_MAIN + p * LANES
        rope = (qs[:, mid:hi_] * gqm_ref[:, mid:hi_]) * cos \
            + (qs[:, xr_lo:xr_lo + LANES] * gqa_ref[:, p * LANES:(p + 1) * LANES]) * sin
        q_ref[:, mid:hi_] = rope.astype(BF16)

    ckv = _rms(_dot(xn, w_in_ref[:, o_kv:o_kp]), kvan_ref[...]).astype(BF16)
    kvraw = _dot(ckv, wkv_ref[...])
    knope = kvraw[:, 0:N_HEADS * QK_NOPE]
    v_ref[...] = kvraw[:, N_HEADS * QK_NOPE:].astype(BF16)
    kpe = _dot(xn, w_in_ref[:, o_kp:o_kr])
    kpe_rot = _dot(xn, w_in_ref[:, o_kr:o_gs])
    ksc = head_scales(jnp.concatenate([knope * knope, kpe * kpe], axis=1), ek_ref, etk_ref)
    krope = (kpe * gkx_ref[...]) * cos + (kpe_rot * gkr_ref[...]) * sin
    kn = knope * ksc[:, 0:N_HEADS * QK_NOPE] * gkn_ref[...]
    for h in range(N_HEADS):
        k_ref[0, h, :, 0:QK_NOPE] = kn[:, h * QK_NOPE:(h + 1) * QK_NOPE].astype(BF16)
        off = N_HEADS * QK_NOPE + h * LANES
        k_ref[0, h, :, QK_NOPE:HEAD_PAD] = (krope * ksc[:, off:off + LANES]).astype(BF16)


def _const_spec(shape):
    nd = len(shape)
    return pl.BlockSpec(shape, lambda *_: (0,) * nd, pipeline_mode=pl.Buffered(1))


def _chunk_rows_spec(tm, nlt):
    return pl.BlockSpec((tm // SSM_CHUNK, None, SSM_CHUNK, SSM_WIDTH), lambda i: (i % nlt, i // nlt, 0, 0))


def _in_proj(x2d, cos, sin, prm, batch, seq):
    t = x2d.shape[0]
    tm = TM_PROJ
    nlt = seq // tm
    row = lambda w: pl.BlockSpec((tm, w), lambda i: (i, 0))
    consts = [prm["nmix"], prm["w_in"], prm["qan"], prm["kvan"], prm["wq"], prm["wkv"],
              prm["gqm"], prm["gqa"], prm["gkn"], prm["gkx"], prm["gkr"],
              prm["eq"], prm["ek"], prm["etq"], prm["etk"]]
    out_shape = [
        jax.ShapeDtypeStruct((seq // SSM_CHUNK, batch, SSM_CHUNK, SSM_WIDTH), F32),
        jax.ShapeDtypeStruct((t, Q_MAIN), BF16),
        jax.ShapeDtypeStruct((batch, N_HEADS, seq, HEAD_PAD), BF16),
        jax.ShapeDtypeStruct((t, N_HEADS * V_HEAD), BF16),
        jax.ShapeDtypeStruct((t, D_MODEL), BF16),
        jax.ShapeDtypeStruct((t, D_MODEL), BF16),
    ]
    out_specs = [
        _chunk_rows_spec(tm, nlt), row(Q_MAIN),
        pl.BlockSpec((1, N_HEADS, tm, HEAD_PAD), lambda i: (i // nlt, 0, i % nlt, 0)),
        row(N_HEADS * V_HEAD), row(D_MODEL), row(D_MODEL),
    ]
    return pl.pallas_call(
        _in_proj_kernel,
        grid=(t // tm,),
        in_specs=[row(D_MODEL), row(LANES), row(LANES)] + [_const_spec(c.shape) for c in consts],
        out_specs=out_specs,
        out_shape=out_shape,
        compiler_params=pltpu.CompilerParams(dimension_semantics=("arbitrary",),
                                             vmem_limit_bytes=VMEM_LIMIT),
        name="in_proj",
    )(x2d, cos, sin, *consts)


def _ssm_kernel(*refs):
    nq = SSM_CHUNK
    u_refs = refs[0:nq]
    wtoe_ref, wst_ref, wout_ref, a_ref, y_ref, s_sc, h_sc, carry_sc, y_sc = refs[nq:]
    rows = s_sc.shape[0]
    n_chunks = rows // SUBLANES

    @pl.when(pl.program_id(1) == 0)
    def _():
        carry_sc[...] = jnp.zeros(carry_sc.shape, F32)

    lhs = jnp.concatenate([u[...] for u in u_refs], axis=1).astype(BF16)
    s_sc[...] = _dot(lhs, wst_ref[0])
    a_re = jnp.broadcast_to(a_ref[0, 0:1, :], (SUBLANES, SSM_ST))
    a_im = jnp.broadcast_to(a_ref[0, 1:2, :], (SUBLANES, SSM_ST))

    def body(k, carry):
        h_re, h_im = carry
        r0 = pl.multiple_of(k * SUBLANES, SUBLANES)
        h_sc[pl.ds(r0, SUBLANES), 0:SSM_ST] = h_re
        h_sc[pl.ds(r0, SUBLANES), SSM_ST:2 * SSM_ST] = h_im
        s_re = s_sc[pl.ds(r0, SUBLANES), 0:SSM_ST]
        s_im = s_sc[pl.ds(r0, SUBLANES), SSM_ST:2 * SSM_ST]
        return (a_re * h_re - a_im * h_im + s_re, a_re * h_im + a_im * h_re + s_im)

    h_re, h_im = lax.fori_loop(0, rows // SUBLANES, body,
                               (carry_sc[:, 0:SSM_ST], carry_sc[:, SSM_ST:2 * SSM_ST]), unroll=4)
    carry_sc[:, 0:SSM_ST] = h_re
    carry_sc[:, SSM_ST:2 * SSM_ST] = h_im

    hb = h_sc[...].astype(BF16)
    for tt in range(nq // 2):
        kk = (2 * tt + 2) * LANES
        c0, c1 = 2 * tt * LANES, (2 * tt + 2) * LANES
        y = _dot(lhs[:, 0:kk], wtoe_ref[0, 0:kk, c0:c1]) + _dot(hb, wout_ref[0, :, c0:c1])
        y_sc[0] = y[:, 0:LANES]
        y_sc[1] = y[:, LANES:2 * LANES]
        for j in range(2):
            for b in range(y_ref.shape[0]):
                y_ref[b, pl.ds(2 * tt + j, n_chunks, stride=SSM_CHUNK), :] = \
                    y_sc[j, pl.ds(b, n_chunks, stride=y_ref.shape[0]), :]


def _ssm(u_rows, prm, batch, seq):
    n_rows = u_rows.shape[0]
    rows = SSM_ROWS
    assert batch == SUBLANES, "one scan step works on a full vreg of batch rows"
    col = lambda s: pl.BlockSpec((rows, LANES), lambda g, r, s=s: (r, s * SSM_TILES + g))
    wspec = lambda: pl.BlockSpec((1, SSM_K, SSM_K), lambda g, r: (g, 0, 0))
    return pl.pallas_call(
        _ssm_kernel,
        grid=(SSM_TILES, n_rows // rows),
        in_specs=[col(s) for s in range(SSM_CHUNK)] + [wspec(), wspec(), wspec(),
                  pl.BlockSpec((1, 2, SSM_ST), lambda g, r: (g, 0, 0))],
        out_specs=pl.BlockSpec((batch, rows, LANES), lambda g, r: (0, r, g)),
        out_shape=jax.ShapeDtypeStruct((batch, seq, SSM_WIDTH), F32),
        scratch_shapes=[pltpu.VMEM((rows, 2 * SSM_ST), F32), pltpu.VMEM((rows, 2 * SSM_ST), F32),
                        pltpu.VMEM((SUBLANES, 2 * SSM_ST), F32), pltpu.VMEM((2, rows, LANES), F32)],
        compiler_params=pltpu.CompilerParams(dimension_semantics=("arbitrary", "arbitrary"),
                                             vmem_limit_bytes=VMEM_LIMIT),
        name="ssm_scan",
    )(*([u_rows] * SSM_CHUNK), prm["ssm_wtoe"], prm["ssm_wst"], prm["ssm_wout"], prm["ssm_a"])


def _attn_kernel(q_ref, k_ref, v_ref, o_ref, *scratch):
    nh = ATTN_HEADS
    m_sc, acc_sc, s_sc, p_sc, alpha_sc = (scratch[nh * n:nh * (n + 1)] for n in range(5))
    i = pl.program_id(2)
    ones = jnp.ones((TK, LANES), BF16)
    nt = (((1,), (1,)), ((), ()))

    def stage_q(kv0):
        for h in range(nh):
            pair = (h // 2) * PAIR_W
            q = jnp.concatenate([q_ref[0, :, pair + (h % 2) * QK_NOPE:pair + (h % 2 + 1) * QK_NOPE],
                                 q_ref[0, :, pair + 2 * QK_NOPE:pair + PAIR_W]], axis=1)
            k = k_ref[0, h, pl.ds(kv0, TK), :]
            s_sc[h][...] = lax.dot_general(q, k, nt, preferred_element_type=F32)

    def stage_p(kv0):
        for h in range(nh):
            v_ext = jnp.concatenate([v_ref[0, pl.ds(kv0, TK), h * V_HEAD:(h + 1) * V_HEAD], ones], axis=1)
            acc_sc[h][...] = acc_sc[h][...] * jnp.tile(alpha_sc[h][...], (1, 2)) + _dot(p_sc[h][...], v_ext)

    def stage_s(first):
        for h in range(nh):
            for r0 in range(0, TQ, ATTN_ROWS):
                rows = slice(r0, r0 + ATTN_ROWS)
                s = s_sc[h][rows, :]
                if first:
                    r = lax.broadcasted_iota(jnp.int32, (ATTN_ROWS, TK), 0) + r0
                    c = lax.broadcasted_iota(jnp.int32, (ATTN_ROWS, TK), 1)
                    s = jnp.where(c <= r, s, -jnp.inf)
                m_prev = m_sc[h][rows, :]
                m_new = jnp.maximum(m_prev, jnp.max(s, axis=-1, keepdims=True))
                alpha_sc[h][rows, :] = jnp.exp2(m_prev - m_new)
                p_sc[h][rows, :] = jnp.exp2(s - jnp.tile(m_new, (1, TK // LANES))).astype(BF16)
                m_sc[h][rows, :] = m_new

    blk = lambda j: pl.multiple_of(j * TK, TK)
    diag = blk(i)
    for h in range(nh):
        m_sc[h][...] = jnp.full((TQ, LANES), -jnp.inf, F32)
        acc_sc[h][...] = jnp.zeros((TQ, 2 * V_HEAD), F32)
    stage_q(diag)
    stage_s(True)

    @pl.when(i > 0)
    def _():
        stage_q(blk(0))

        def body(t, carry):
            stage_p(pl.multiple_of(jnp.where(t == 1, diag, blk(t - 2)), TK))
            stage_s(False)
            stage_q(blk(t))
            return carry

        lax.fori_loop(1, i, body, 0)
        stage_p(pl.multiple_of(jnp.where(i == 1, diag, blk(i - 2)), TK))
        stage_s(False)

    stage_p(blk(jnp.maximum(i - 1, 0)))
    for h in range(nh):
        acc = acc_sc[h][...]
        o_ref[0, :, h * V_HEAD:(h + 1) * V_HEAD] = (acc[:, 0:V_HEAD] / acc[:, V_HEAD:2 * V_HEAD]).astype(BF16)


def _attention(q, k, v, batch, seq):
    assert TQ == TK
    nh = ATTN_HEADS
    per_head = lambda shape, dt: [pltpu.VMEM(shape, dt)] * nh
    return pl.pallas_call(
        _attn_kernel,
        grid=(batch, N_HEADS // nh, seq // TQ),
        in_specs=[pl.BlockSpec((1, TQ, nh // 2 * PAIR_W), lambda b, g, i: (b, i, g)),
                  pl.BlockSpec((1, nh, seq, HEAD_PAD), lambda b, g, i: (b, g, 0, 0)),
                  pl.BlockSpec((1, seq, nh * V_HEAD), lambda b, g, i: (b, 0, g))],
        out_specs=pl.BlockSpec((1, TQ, nh * V_HEAD), lambda b, g, i: (b, i, g)),
        out_shape=jax.ShapeDtypeStruct((batch, seq, N_HEADS * V_HEAD), BF16),
        scratch_shapes=per_head((TQ, LANES), F32) + per_head((TQ, 2 * V_HEAD), F32)
                       + per_head((TQ, TK), F32) + per_head((TQ, TK), BF16) + per_head((TQ, LANES), F32),
        compiler_params=pltpu.CompilerParams(dimension_semantics=("arbitrary",) * 3,
                                             vmem_limit_bytes=VMEM_LIMIT),
        name="mla_attention",
    )(q, k, v)


def _post_kernel(x_ref, y_ref, attn_ref, sgs_ref, sgm_ref, wglu_ref, bglu_ref, wos_ref, wom_ref, wout_ref,
                 nmlp_ref, wup_ref, wdown_ref, o_ref):
    z = jax.nn.gelu(y_ref[...])
    z = z * jax.nn.sigmoid(_dot(z.astype(BF16), wglu_ref[...]) + bglu_ref[...])
    y_ssm = _dot(z.astype(BF16), wos_ref[...])
    y_mla = _dot(attn_ref[...], wom_ref[...])
    mixed = sgs_ref[...].astype(F32) * y_ssm + sgm_ref[...].astype(F32) * y_mla
    h = x_ref[...] + _dot(mixed.astype(BF16), wout_ref[...])
    hn = _rms(h, nmlp_ref[...]).astype(BF16)
    acc = h
    for c in range(D_FF // FF_CHUNK):
        up = _dot(hn, wup_ref[:, c * FF_CHUNK:(c + 1) * FF_CHUNK])
        hid = jnp.square(jnp.maximum(up, 0.0)).astype(BF16)
        acc = acc + _dot(hid, wdown_ref[c * FF_CHUNK:(c + 1) * FF_CHUNK, :])
    o_ref[...] = acc


def _post(x2d, y2d, attn2d, sgs, sgm, prm):
    t = x2d.shape[0]
    tm = TM_POST
    row = lambda w: pl.BlockSpec((tm, w), lambda i: (i, 0))
    consts = [prm["w_glu"], prm["b_glu"], prm["w_o_ssm"], prm["w_o_mla"], prm["w_out"],
              prm["nmlp"], prm["w_up"], prm["w_down"]]
    return pl.pallas_call(
        _post_kernel,
        grid=(t // tm,),
        in_specs=[row(D_MODEL), row(SSM_WIDTH), row(D_MODEL), row(D_MODEL), row(D_MODEL)]
                 + [_const_spec(c.shape) for c in consts],
        out_specs=row(D_MODEL),
        out_shape=jax.ShapeDtypeStruct((t, D_MODEL), F32),
        compiler_params=pltpu.CompilerParams(dimension_semantics=("arbitrary",),
                                             vmem_limit_bytes=VMEM_LIMIT),
        name="merge_mlp",
    )(x2d, y2d, attn2d, sgs, sgm, *consts)


def _rot_cols(w):
    half = w.shape[-1] // 2
    return jnp.concatenate([-w[..., half:], w[..., :half]], axis=-1)


def _swap_halves(g):
    half = g.shape[-1] // 2
    return jnp.concatenate([g[..., half:], g[..., :half]], axis=-1)


def _head_indicators():
    eq = np.zeros((Q_MAIN, LANES), np.float32)
    etq = np.zeros((2 * LANES, Q_MAIN + Q_AUX), np.float32)
    for p in range(N_PAIRS):
        for hh in range(2):
            h = 2 * p + hh
            cols = list(range(p * PAIR_W + hh * QK_NOPE, p * PAIR_W + (hh + 1) * QK_NOPE))
            cols += list(range(p * PAIR_W + 2 * QK_NOPE + hh * QK_ROPE, p * PAIR_W + 2 * QK_NOPE + (hh + 1) * QK_ROPE))
            eq[cols, h] = 1.0
            aux = list(range(Q_MAIN + p * LANES + hh * QK_ROPE, Q_MAIN + p * LANES + (hh + 1) * QK_ROPE))
            for base in (0, LANES):
                etq[base + h, cols + aux] = 1.0
    nk = N_HEADS * QK_NOPE
    ek = np.zeros((nk + LANES, LANES), np.float32)
    etk = np.zeros((2 * LANES, nk + N_HEADS * LANES), np.float32)
    for h in range(N_HEADS):
        ek[h * QK_NOPE:(h + 1) * QK_NOPE, h] = 1.0
        ek[nk:, h] = 0.5
        half0 = nk + h * LANES + (h % 2) * QK_ROPE
        for base in (0, LANES):
            etk[base + h, h * QK_NOPE:(h + 1) * QK_NOPE] = 1.0
            etk[base + h, half0:half0 + QK_ROPE] = 1.0
    as_bf = lambda a: jnp.asarray(a, BF16)
    return as_bf(eq), as_bf(ek), as_bf(etq), as_bf(etk)


def _block_diag_expand(compact, row_div, col_div, rep):
    r = (np.arange(SSM_K) // row_div) % GROUPS_PER_TILE
    c = (np.arange(SSM_K) // col_div) % GROUPS_PER_TILE
    mask = jnp.asarray(r[:, None] == c[None, :], F32)
    wide = jnp.einsum("grk,kc->grc", compact, jnp.asarray(rep, F32))
    return (wide * mask).astype(BF16)


def _ssm_params(a_re, a_im, log_dt, b_re, b_im, c_re, c_im, d_skip):
    q, g8, nt = SSM_CHUNK, GROUPS_PER_TILE, SSM_TILES
    hp = lax.Precision.HIGHEST
    ar, ai = a_re.astype(F32), a_im.astype(F32)
    dt = jnp.exp(log_dt.astype(F32))[:, None]
    steps = jnp.arange(q + 1, dtype=F32)[None, :, None]
    mag = jnp.exp((ar * dt)[:, None, :] * steps)
    ang = (ai * dt)[:, None, :] * steps
    pr, pi = mag * jnp.cos(ang), mag * jnp.sin(ang)
    x, y = pr[:, 1] - 1.0, pi[:, 1]
    den = ar * ar + ai * ai
    cr, ci = (x * ar + y * ai) / den, (y * ar - x * ai) / den
    br, bi = b_re.astype(F32), b_im.astype(F32)
    bbr = cr[..., None] * br - ci[..., None] * bi
    bbi = cr[..., None] * bi + ci[..., None] * br
    cre, cim = c_re.astype(F32), c_im.astype(F32)
    cpr = cre[:, None] * pr[:, :, None, :] - cim[:, None] * pi[:, :, None, :]
    cpi = cre[:, None] * pi[:, :, None, :] + cim[:, None] * pr[:, :, None, :]
    kj = (jnp.einsum("gjcp,gpi->gjci", cpr[:, :q], bbr, precision=hp)
          - jnp.einsum("gjcp,gpi->gjci", cpi[:, :q], bbi, precision=hp))
    kj = kj.at[:, 0].add(jax.vmap(jnp.diag)(d_skip.astype(F32)))
    s_idx = np.arange(q)[:, None]
    t_idx = np.arange(q)[None, :]
    toe = kj[:, np.clip(t_idx - s_idx, 0, q - 1)]
    toe = jnp.where((t_idx >= s_idx)[None, :, :, None, None], toe, 0.0)
    toe = jnp.transpose(toe, (0, 1, 4, 2, 3))
    rev = q - 1 - np.arange(q)
    pr_s, pi_s = pr[:, rev][..., None], pi[:, rev][..., None]
    st_re = jnp.swapaxes(pr_s * bbr[:, None] - pi_s * bbi[:, None], 2, 3)
    st_im = jnp.swapaxes(pr_s * bbi[:, None] + pi_s * bbr[:, None], 2, 3)
    wst = jnp.stack([st_re, st_im], axis=3)
    cl = jnp.stack([cpr[:, 1:], -cpi[:, 1:]], axis=1)
    cl = jnp.transpose(cl, (0, 1, 4, 2, 3))

    def tile_major(w, lead):
        w = w.reshape((nt, g8) + lead + (LANES,))
        return jnp.swapaxes(w, 1, 2).reshape(nt, SSM_K, LANES)

    toe_c = tile_major(toe.reshape(SSM_GROUPS, q, SSM_GROUP_CH, LANES), (q, SSM_GROUP_CH))
    wst_c = tile_major(wst.reshape(SSM_GROUPS, q, SSM_GROUP_CH, LANES), (q, SSM_GROUP_CH))
    out_c = tile_major(cl.reshape(SSM_GROUPS, 2, SSM_STATE, LANES), (2, SSM_STATE))
    cols = np.arange(SSM_K)
    rep_tc = (np.arange(LANES)[:, None] == ((cols // LANES) * SSM_GROUP_CH + cols % SSM_GROUP_CH)[None, :])
    rep_rp = (np.arange(LANES)[:, None] == ((cols // SSM_ST) * SSM_STATE + cols % SSM_STATE)[None, :])
    wtoe = _block_diag_expand(toe_c, SSM_GROUP_CH, SSM_GROUP_CH, rep_tc)
    wst = _block_diag_expand(wst_c, SSM_GROUP_CH, SSM_STATE, rep_rp)
    wout = _block_diag_expand(out_c, SSM_STATE, SSM_GROUP_CH, rep_tc)
    a_rows = jnp.stack([pr[:, q].reshape(nt, SSM_ST), pi[:, q].reshape(nt, SSM_ST)], axis=1)
    return wtoe, wst, wout, a_rows


def _prepare_params(norm_mix, w_in, q_a_norm, kv_a_norm, w_q_b, w_kv_b, q_norm, k_norm, w_o_mla,
                    ssm_a_re, ssm_a_im, ssm_log_dt, ssm_b_re, ssm_b_im, ssm_c_re, ssm_c_im, ssm_d,
                    w_glu, b_glu, w_o_ssm, w_out, norm_mlp, w_up, w_down):
    o_q = SSM_WIDTH
    o_kv = o_q + Q_LORA
    o_kp = o_kv + KV_LORA
    o_gs = o_kp + QK_ROPE
    w_kpe = w_in[:, o_kp:o_gs]
    w_kpr = _rot_cols(w_kpe)
    w_in2 = jnp.concatenate([w_in[:, :o_kp], w_kpe, w_kpe, w_kpr, w_kpr, w_in[:, o_gs:]], axis=1).astype(BF16)

    wq = w_q_b.reshape(Q_LORA, N_HEADS, QK_HEAD)
    main, aux, gqm, gqa = [], [], [], []
    qscale = (QK_HEAD ** -0.5) * math.log2(math.e)
    g_nope, g_rope = q_norm[:QK_NOPE] * qscale, q_norm[QK_NOPE:] * qscale
    for p in range(N_PAIRS):
        a, b = 2 * p, 2 * p + 1
        main += [wq[:, a, :QK_NOPE], wq[:, b, :QK_NOPE], wq[:, a, QK_NOPE:], wq[:, b, QK_NOPE:]]
        aux += [_rot_cols(wq[:, a, QK_NOPE:]), _rot_cols(wq[:, b, QK_NOPE:])]
        gqm += [g_nope, g_nope, g_rope, g_rope]
        gqa += [_swap_halves(g_rope)] * 2
    wq2 = jnp.concatenate(main + aux, axis=1).astype(BF16)

    wkv = w_kv_b.reshape(KV_LORA, N_HEADS, QK_NOPE + V_HEAD)
    wkv2 = jnp.concatenate([wkv[:, :, :QK_NOPE].reshape(KV_LORA, -1),
                            wkv[:, :, QK_NOPE:].reshape(KV_LORA, -1)], axis=1).astype(BF16)
    gk_rope = k_norm[QK_NOPE:]
    eq, ek, etq, etk = _head_indicators()
    wtoe, wst, wout_ssm, a_rows = _ssm_params(ssm_a_re, ssm_a_im, ssm_log_dt, ssm_b_re, ssm_b_im,
                                              ssm_c_re, ssm_c_im, ssm_d)
    row = lambda v: v.astype(F32).reshape(1, -1)
    return {
        "nmix": row(norm_mix), "w_in": w_in2, "qan": row(q_a_norm), "kvan": row(kv_a_norm),
        "wq": wq2, "wkv": wkv2,
        "gqm": row(jnp.concatenate(gqm)), "gqa": row(jnp.concatenate(gqa)),
        "gkn": row(jnp.tile(k_norm[:QK_NOPE], N_HEADS)),
        "gkx": row(jnp.tile(gk_rope, 2)), "gkr": row(jnp.tile(_swap_halves(gk_rope), 2)),
        "eq": eq, "ek": ek, "etq": etq, "etk": etk,
        "ssm_wtoe": wtoe, "ssm_wst": wst, "ssm_wout": wout_ssm, "ssm_a": a_rows,
        "w_glu": w_glu.astype(BF16), "b_glu": row(b_glu), "w_o_ssm": w_o_ssm.astype(BF16),
        "w_o_mla": w_o_mla.astype(BF16), "w_out": w_out.astype(BF16), "nmlp": row(norm_mlp),
        "w_up": w_up.astype(BF16), "w_down": w_down.astype(BF16),
    }


def _layer(h, cos, sin, prm):
    batch, seq, _ = h.shape
    t = batch * seq
    x2d = h.reshape(t, D_MODEL)
    u, q, k, v, sgs, sgm = _in_proj(x2d, cos, sin, prm, batch, seq)
    n_rows = (seq // SSM_CHUNK) * batch
    y = _ssm(u.reshape(n_rows, SSM_CHUNK * SSM_WIDTH), prm, batch, seq)
    attn = _attention(q.reshape(batch, seq, Q_MAIN), k, v.reshape(batch, seq, N_HEADS * V_HEAD), batch, seq)
    out = _post(x2d, y.reshape(t, SSM_WIDTH), attn.reshape(t, N_HEADS * V_HEAD), sgs, sgm, prm)
    return out.reshape(batch, seq, D_MODEL)


def kernel(x, positions, norm_mix, w_in, q_a_norm, kv_a_norm, w_q_b, w_kv_b, q_norm, k_norm, w_o_mla,
           ssm_a_re, ssm_a_im, ssm_log_dt, ssm_b_re, ssm_b_im, ssm_c_re, ssm_c_im, ssm_d,
           w_glu, b_glu, w_o_ssm, w_out, norm_mlp, w_up, w_down):
    params = (norm_mix, w_in, q_a_norm, kv_a_norm, w_q_b, w_kv_b, q_norm, k_norm, w_o_mla,
              ssm_a_re, ssm_a_im, ssm_log_dt, ssm_b_re, ssm_b_im, ssm_c_re, ssm_c_im, ssm_d,
              w_glu, b_glu, w_o_ssm, w_out, norm_mlp, w_up, w_down)
    cos, sin = _rope_tables(positions)
    h = x
    for layer in range(norm_mix.shape[0]):
        h = _layer(h, cos, sin, _prepare_params(*[p[layer] for p in params]))
    return h
```

```python
import functools
import math

import numpy as np
import jax
import jax.numpy as jnp
from jax import lax
from jax.experimental import pallas as pl
from jax.experimental.pallas import tpu as pltpu

D_MODEL = 1024
SSM_GROUPS = 32
SSM_GROUP_CH = 16
SSM_WIDTH = SSM_GROUPS * SSM_GROUP_CH
SSM_STATE = 64
N_HEADS = 8
N_PAIRS = N_HEADS // 2
QK_NOPE = 128
QK_ROPE = 64
QK_HEAD = QK_NOPE + QK_ROPE
V_HEAD = 128
Q_LORA = 384
KV_LORA = 256
ROPE_THETA = 10000.0
D_FF = 4 * D_MODEL
EPS = 1e-6

LANES = 128
SUBLANES = 8
SSM_CHUNK = SUBLANES
GROUPS_PER_TILE = LANES // SSM_GROUP_CH
SSM_TILES = SSM_WIDTH // LANES
SSM_K = SSM_CHUNK * LANES
SSM_ST = GROUPS_PER_TILE * SSM_STATE
SSM_ROWS = 1024
HEAD_PAD = 256
PAIR_W = 2 * QK_NOPE + LANES
Q_MAIN = N_PAIRS * PAIR_W
Q_AUX = N_PAIRS * LANES
VMEM_LIMIT = 56 * 1024 * 1024

TM_PROJ = 512
TM_POST = 512
TQ = 512
TK = 512
ATTN_ROWS = 64
ATTN_HEADS = 4
FF_CHUNK = 1024

F32 = jnp.float32
BF16 = jnp.bfloat16


def _dot(a, b):
    return jnp.dot(a, b, preferred_element_type=F32)


def _rms(x, gain):
    ms = jnp.mean(x * x, axis=-1, keepdims=True)
    return x * lax.rsqrt(ms + EPS) * gain


def _rope_kernel(pos_ref, freq_ref, cos_ref, sin_ref):
    ang = freq_ref[...] * pos_ref[0].astype(F32)
    cos_ref[0] = jnp.cos(ang)
    sin_ref[0] = jnp.sin(ang)


def _rope_tables(positions):
    b, l = positions.shape
    half = QK_ROPE // 2
    inv_freq = ROPE_THETA ** (-jnp.arange(half, dtype=F32) / half)
    out = jax.ShapeDtypeStruct((b, half, l), F32)
    cos_t, sin_t = pl.pallas_call(
        _rope_kernel,
        grid=(b,),
        in_specs=[pl.BlockSpec((1, 1, l), lambda i: (i, 0, 0)),
                  pl.BlockSpec((half, 1), lambda i: (0, 0))],
        out_specs=[pl.BlockSpec((1, half, l), lambda i: (i, 0, 0))] * 2,
        out_shape=[out, out],
        name="rope_tables",
    )(positions.reshape(b, 1, l), inv_freq.reshape(half, 1))

    def expand(t):
        t = jnp.transpose(t, (0, 2, 1))
        return jnp.tile(t, (1, 1, LANES // half)).reshape(b * l, LANES)

    return expand(cos_t), expand(sin_t)


def _in_proj_kernel(x_ref, cos_ref, sin_ref, nmix_ref, w_in_ref, qan_ref, kvan_ref, wq_ref, wkv_ref,
                    gqm_ref, gqa_ref, gkn_ref, gkx_ref, gkr_ref,
                    u_ref, q_ref, k_ref, v_ref, sgs_ref, sgm_ref):
    o_q = SSM_WIDTH
    o_kv = Q_LORA
    o_kp = o_kv + KV_LORA
    o_kr = o_kp + LANES
    o_gs = o_q + o_kr + LANES
    o_gm = o_gs + D_MODEL
    tm = x_ref.shape[0]

    xn = _rms(x_ref[...], nmix_ref[...]).astype(BF16)
    cos = cos_ref[...]
    sin = sin_ref[...]

    u_ref[...] = _dot(xn, w_in_ref[:, 0:o_q]).reshape(u_ref.shape)
    sgs_ref[...] = jax.nn.sigmoid(_dot(xn, w_in_ref[:, o_gs:o_gm])).astype(BF16)
    sgm_ref[...] = jax.nn.sigmoid(_dot(xn, w_in_ref[:, o_gm:o_gm + D_MODEL])).astype(BF16)
    lat = _dot(xn, w_in_ref[:, o_q:o_gs])

    first_half = lax.broadcasted_iota(jnp.int32, (tm, LANES), 1) < QK_ROPE

    def inv_rms(sq):
        return lax.rsqrt(jnp.sum(sq, axis=-1, keepdims=True) * (1.0 / QK_HEAD) + EPS)

    qn = _rms(lat[:, 0:o_kv], qan_ref[...]).astype(BF16)
    qraw = _dot(qn, wq_ref[...])
    for p in range(N_PAIRS):
        c0 = p * PAIR_W
        na, nb = qraw[:, c0:c0 + QK_NOPE], qraw[:, c0 + QK_NOPE:c0 + 2 * QK_NOPE]
        rp = qraw[:, c0 + 2 * QK_NOPE:c0 + PAIR_W]
        rr = qraw[:, Q_MAIN + p * LANES:Q_MAIN + (p + 1) * LANES]
        rp2 = rp * rp
        inv_a = inv_rms(na * na + jnp.where(first_half, rp2, 0.0))
        inv_b = inv_rms(nb * nb + jnp.where(first_half, 0.0, rp2))
        inv_r = jnp.where(first_half, inv_a, inv_b)
        q_ref[:, c0:c0 + QK_NOPE] = (na * inv_a * gqm_ref[:, c0:c0 + QK_NOPE]).astype(BF16)
        q_ref[:, c0 + QK_NOPE:c0 + 2 * QK_NOPE] = (nb * inv_b * gqm_ref[:, c0 + QK_NOPE:c0 + 2 * QK_NOPE]).astype(BF16)
        rope = (rp * inv_r * gqm_ref[:, c0 + 2 * QK_NOPE:c0 + PAIR_W]) * cos \
            + (rr * inv_r * gqa_ref[:, p * LANES:(p + 1) * LANES]) * sin
        q_ref[:, c0 + 2 * QK_NOPE:c0 + PAIR_W] = rope.astype(BF16)

    ckv = _rms(lat[:, o_kv:o_kp], kvan_ref[...]).astype(BF16)
    kvraw = _dot(ckv, wkv_ref[...])
    v_ref[...] = kvraw[:, N_HEADS * QK_NOPE:].astype(BF16)
    kpe = lat[:, o_kp:o_kr]
    kpe_rot = lat[:, o_kr:o_kr + LANES]
    kpe_sq = kpe * kpe * 0.5
    krope = (kpe * gkx_ref[...]) * cos + (kpe_rot * gkr_ref[...]) * sin
    for h in range(N_HEADS):
        kn = kvraw[:, h * QK_NOPE:(h + 1) * QK_NOPE]
        inv = inv_rms(kn * kn + kpe_sq)
        k_ref[0, h, :, 0:QK_NOPE] = (kn * inv * gkn_ref[:, h * QK_NOPE:(h + 1) * QK_NOPE]).astype(BF16)
        own_half = first_half if h % 2 == 0 else jnp.logical_not(first_half)
        k_ref[0, h, :, QK_NOPE:HEAD_PAD] = jnp.where(own_half, krope * inv, 0.0).astype(BF16)


def _const_spec(shape):
    nd = len(shape)
    return pl.BlockSpec(shape, lambda *_: (0,) * nd, pipeline_mode=pl.Buffered(1))


def _chunk_rows_spec(tm, nlt):
    return pl.BlockSpec((tm // SSM_CHUNK, None, SSM_CHUNK, SSM_WIDTH), lambda i: (i % nlt, i // nlt, 0, 0))


def _in_proj(x2d, cos, sin, prm, batch, seq):
    t = x2d.shape[0]
    tm = TM_PROJ
    nlt = seq // tm
    row = lambda w: pl.BlockSpec((tm, w), lambda i: (i, 0))
    consts = [prm["nmix"], prm["w_in"], prm["qan"], prm["kvan"], prm["wq"], prm["wkv"],
              prm["gqm"], prm["gqa"], prm["gkn"], prm["gkx"], prm["gkr"]]
    out_shape = [
        jax.ShapeDtypeStruct((seq // SSM_CHUNK, batch, SSM_CHUNK, SSM_WIDTH), F32),
        jax.ShapeDtypeStruct((t, Q_MAIN), BF16),
        jax.ShapeDtypeStruct((batch, N_HEADS, seq, HEAD_PAD), BF16),
        jax.ShapeDtypeStruct((t, N_HEADS * V_HEAD), BF16),
        jax.ShapeDtypeStruct((t, D_MODEL), BF16),
        jax.ShapeDtypeStruct((t, D_MODEL), BF16),
    ]
    out_specs = [
        _chunk_rows_spec(tm, nlt), row(Q_MAIN),
        pl.BlockSpec((1, N_HEADS, tm, HEAD_PAD), lambda i: (i // nlt, 0, i % nlt, 0)),
        row(N_HEADS * V_HEAD), row(D_MODEL), row(D_MODEL),
    ]
    return pl.pallas_call(
        _in_proj_kernel,
        grid=(t // tm,),
        in_specs=[row(D_MODEL), row(LANES), row(LANES)] + [_const_spec(c.shape) for c in consts],
        out_specs=out_specs,
        out_shape=out_shape,
        compiler_params=pltpu.CompilerParams(dimension_semantics=("arbitrary",),
                                             vmem_limit_bytes=VMEM_LIMIT),
        name="in_proj",
    )(x2d, cos, sin, *consts)


def _ssm_kernel(*refs):
    nq = SSM_CHUNK
    u_refs = refs[0:nq]
    wtoe_ref, wst_ref, wout_ref, a_ref, y_ref, s_sc, h_sc, carry_sc, y_sc = refs[nq:]
    rows = s_sc.shape[0]
    n_chunks = rows // SUBLANES

    @pl.when(pl.program_id(1) == 0)
    def _():
        carry_sc[...] = jnp.zeros(carry_sc.shape, F32)

    lhs = jnp.concatenate([u[...] for u in u_refs], axis=1).astype(BF16)
    s_sc[...] = _dot(lhs, wst_ref[0])
    a_re = jnp.broadcast_to(a_ref[0, 0:1, :], (SUBLANES, SSM_ST))
    a_im = jnp.broadcast_to(a_ref[0, 1:2, :], (SUBLANES, SSM_ST))

    def body(k, carry):
        h_re, h_im = carry
        r0 = pl.multiple_of(k * SUBLANES, SUBLANES)
        h_sc[pl.ds(r0, SUBLANES), 0:SSM_ST] = h_re
        h_sc[pl.ds(r0, SUBLANES), SSM_ST:2 * SSM_ST] = h_im
        s_re = s_sc[pl.ds(r0, SUBLANES), 0:SSM_ST]
        s_im = s_sc[pl.ds(r0, SUBLANES), SSM_ST:2 * SSM_ST]
        return (a_re * h_re - a_im * h_im + s_re, a_re * h_im + a_im * h_re + s_im)

    h_re, h_im = lax.fori_loop(0, rows // SUBLANES, body,
                               (carry_sc[:, 0:SSM_ST], carry_sc[:, SSM_ST:2 * SSM_ST]), unroll=4)
    carry_sc[:, 0:SSM_ST] = h_re
    carry_sc[:, SSM_ST:2 * SSM_ST] = h_im

    hb = h_sc[...].astype(BF16)
    for tt in range(nq // 2):
        kk = (2 * tt + 2) * LANES
        c0, c1 = 2 * tt * LANES, (2 * tt + 2) * LANES
        y = _dot(lhs[:, 0:kk], wtoe_ref[0, 0:kk, c0:c1]) + _dot(hb, wout_ref[0, :, c0:c1])
        y_sc[0] = y[:, 0:LANES]
        y_sc[1] = y[:, LANES:2 * LANES]
        for j in range(2):
            for b in range(y_ref.shape[0]):
                y_ref[b, pl.ds(2 * tt + j, n_chunks, stride=SSM_CHUNK), :] = \
                    y_sc[j, pl.ds(b, n_chunks, stride=y_ref.shape[0]), :]


def _ssm(u_rows, prm, batch, seq):
    n_rows = u_rows.shape[0]
    rows = SSM_ROWS
    assert batch == SUBLANES, "one scan step works on a full vreg of batch rows"
    col = lambda s: pl.BlockSpec((rows, LANES), lambda g, r, s=s: (r, s * SSM_TILES + g))
    wspec = lambda: pl.BlockSpec((1, SSM_K, SSM_K), lambda g, r: (g, 0, 0))
    return pl.pallas_call(
        _ssm_kernel,
        grid=(SSM_TILES, n_rows // rows),
        in_specs=[col(s) for s in range(SSM_CHUNK)] + [wspec(), wspec(), wspec(),
                  pl.BlockSpec((1, 2, SSM_ST), lambda g, r: (g, 0, 0))],
        out_specs=pl.BlockSpec((batch, rows, LANES), lambda g, r: (0, r, g)),
        out_shape=jax.ShapeDtypeStruct((batch, seq, SSM_WIDTH), F32),
        scratch_shapes=[pltpu.VMEM((rows, 2 * SSM_ST), F32), pltpu.VMEM((rows, 2 * SSM_ST), F32),
                        pltpu.VMEM((SUBLANES, 2 * SSM_ST), F32), pltpu.VMEM((2, rows, LANES), F32)],
        compiler_params=pltpu.CompilerParams(dimension_semantics=("arbitrary", "arbitrary"),
                                             vmem_limit_bytes=VMEM_LIMIT),
        name="ssm_scan",
    )(*([u_rows] * SSM_CHUNK), prm["ssm_wtoe"], prm["ssm_wst"], prm["ssm_wout"], prm["ssm_a"])


def _attn_kernel(q_ref, k_ref, v_ref, o_ref, *scratch):
    nh = ATTN_HEADS
    m_sc, acc_sc, s_sc, p_sc, alpha_sc = (scratch[nh * n:nh * (n + 1)] for n in range(5))
    i = pl.program_id(2)
    ones = jnp.ones((TK, LANES), BF16)
    nt = (((1,), (1,)), ((), ()))

    def stage_q(kv0):
        for h in range(nh):
            pair = (h // 2) * PAIR_W
            q = jnp.concatenate([q_ref[0, :, pair + (h % 2) * QK_NOPE:pair + (h % 2 + 1) * QK_NOPE],
                                 q_ref[0, :, pair + 2 * QK_NOPE:pair + PAIR_W]], axis=1)
            k = k_ref[0, h, pl.ds(kv0, TK), :]
            s_sc[h][...] = lax.dot_general(q, k, nt, preferred_element_type=F32)

    def stage_p(kv0):
        for h in range(nh):
            v_ext = jnp.concatenate([v_ref[0, pl.ds(kv0, TK), h * V_HEAD:(h + 1) * V_HEAD], ones], axis=1)
            acc_sc[h][...] = acc_sc[h][...] * jnp.tile(alpha_sc[h][...], (1, 2)) + _dot(p_sc[h][...], v_ext)

    def stage_s(first):
        for h in range(nh):
            for r0 in range(0, TQ, ATTN_ROWS):
                rows = slice(r0, r0 + ATTN_ROWS)
                s = s_sc[h][rows, :]
                if first:
                    r = lax.broadcasted_iota(jnp.int32, (ATTN_ROWS, TK), 0) + r0
                    c = lax.broadcasted_iota(jnp.int32, (ATTN_ROWS, TK), 1)
                    s = jnp.where(c <= r, s, -jnp.inf)
                m_prev = m_sc[h][rows, :]
                m_new = jnp.maximum(m_prev, jnp.max(s, axis=-1, keepdims=True))
                alpha_sc[h][rows, :] = jnp.exp2(m_prev - m_new)
                p_sc[h][rows, :] = jnp.exp2(s - jnp.tile(m_new, (1, TK // LANES))).astype(BF16)
                m_sc[h][rows, :] = m_new

    blk = lambda j: pl.multiple_of(j * TK, TK)
    diag = blk(i)
    for h in range(nh):
        m_sc[h][...] = jnp.full((TQ, LANES), -jnp.inf, F32)
        acc_sc[h][...] = jnp.zeros((TQ, 2 * V_HEAD), F32)
    stage_q(diag)
    stage_s(True)

    @pl.when(i > 0)
    def _():
        stage_q(blk(0))

        def body(t, carry):
            stage_p(pl.multiple_of(jnp.where(t == 1, diag, blk(t - 2)), TK))
            stage_s(False)
            stage_q(blk(t))
            return carry

        lax.fori_loop(1, i, body, 0)
        stage_p(pl.multiple_of(jnp.where(i == 1, diag, blk(i - 2)), TK))
        stage_s(False)

    stage_p(blk(jnp.maximum(i - 1, 0)))
    for h in range(nh):
        acc = acc_sc[h][...]
        o_ref[0, :, h * V_HEAD:(h + 1) * V_HEAD] = (acc[:, 0:V_HEAD] / acc[:, V_HEAD:2 * V_HEAD]).astype(BF16)


def _attention(q, k, v, batch, seq):
    assert TQ == TK
    nh = ATTN_HEADS
    per_head = lambda shape, dt: [pltpu.VMEM(shape, dt)] * nh
    return pl.pallas_call(
        _attn_kernel,
        grid=(batch, N_HEADS // nh, seq // TQ),
        in_specs=[pl.BlockSpec((1, TQ, nh // 2 * PAIR_W), lambda b, g, i: (b, i, g)),
                  pl.BlockSpec((1, nh, seq, HEAD_PAD), lambda b, g, i: (b, g, 0, 0)),
                  pl.BlockSpec((1, seq, nh * V_HEAD), lambda b, g, i: (b, 0, g))],
        out_specs=pl.BlockSpec((1, TQ, nh * V_HEAD), lambda b, g, i: (b, i, g)),
        out_shape=jax.ShapeDtypeStruct((batch, seq, N_HEADS * V_HEAD), BF16),
        scratch_shapes=per_head((TQ, LANES), F32) + per_head((TQ, 2 * V_HEAD), F32)
                       + per_head((TQ, TK), F32) + per_head((TQ, TK), BF16) + per_head((TQ, LANES), F32),
        compiler_params=pltpu.CompilerParams(dimension_semantics=("arbitrary",) * 3,
                                             vmem_limit_bytes=VMEM_LIMIT),
        name="mla_attention",
    )(q, k, v)


def _post_kernel(x_ref, y_ref, attn_ref, sgs_ref, sgm_ref, wglu_ref, bglu_ref, wos_ref, wom_ref, wout_ref,
                 nmlp_ref, wup_ref, wdown_ref, o_ref):
    z = jax.nn.gelu(y_ref[...])
    z = z * jax.nn.sigmoid(_dot(z.astype(BF16), wglu_ref[...]) + bglu_ref[...])
    y_ssm = _dot(z.astype(BF16), wos_ref[...])
    y_mla = _dot(attn_ref[...], wom_ref[...])
    mixed = sgs_ref[...].astype(F32) * y_ssm + sgm_ref[...].astype(F32) * y_mla
    h = x_ref[...] + _dot(mixed.astype(BF16), wout_ref[...])
    hn = _rms(h, nmlp_ref[...]).astype(BF16)
    acc = h
    for c in range(D_FF // FF_CHUNK):
        up = _dot(hn, wup_ref[:, c * FF_CHUNK:(c + 1) * FF_CHUNK])
        hid = jnp.square(jnp.maximum(up, 0.0)).astype(BF16)
        acc = acc + _dot(hid, wdown_ref[c * FF_CHUNK:(c + 1) * FF_CHUNK, :])
    o_ref[...] = acc


def _post(x2d, y2d, attn2d, sgs, sgm, prm):
    t = x2d.shape[0]
    tm = TM_POST
    row = lambda w: pl.BlockSpec((tm, w), lambda i: (i, 0))
    consts = [prm["w_glu"], prm["b_glu"], prm["w_o_ssm"], prm["w_o_mla"], prm["w_out"],
              prm["nmlp"], prm["w_up"], prm["w_down"]]
    return pl.pallas_call(
        _post_kernel,
        grid=(t // tm,),
        in_specs=[row(D_MODEL), row(SSM_WIDTH), row(D_MODEL), row(D_MODEL), row(D_MODEL)]
                 + [_const_spec(c.shape) for c in consts],
        out_specs=row(D_MODEL),
        out_shape=jax.ShapeDtypeStruct((t, D_MODEL), F32),
        compiler_params=pltpu.CompilerParams(dimension_semantics=("arbitrary",),
                                             vmem_limit_bytes=VMEM_LIMIT),
        name="merge_mlp",
    )(x2d, y2d, attn2d, sgs, sgm, *consts)


def _rot_cols(w):
    half = w.shape[-1] // 2
    return jnp.concatenate([-w[..., half:], w[..., :half]], axis=-1)


def _swap_halves(g):
    half = g.shape[-1] // 2
    return jnp.concatenate([g[..., half:], g[..., :half]], axis=-1)


def _block_diag_expand(compact, row_div, col_div, rep):
    r = (np.arange(SSM_K) // row_div) % GROUPS_PER_TILE
    c = (np.arange(SSM_K) // col_div) % GROUPS_PER_TILE
    mask = jnp.asarray(r[:, None] == c[None, :], F32)
    wide = jnp.einsum("grk,kc->grc", compact, jnp.asarray(rep, F32))
    return (wide * mask).astype(BF16)


def _ssm_params(a_re, a_im, log_dt, b_re, b_im, c_re, c_im, d_skip):
    q, g8, nt = SSM_CHUNK, GROUPS_PER_TILE, SSM_TILES
    hp = lax.Precision.HIGHEST
    ar, ai = a_re.astype(F32), a_im.astype(F32)
    dt = jnp.exp(log_dt.astype(F32))[:, None]
    steps = jnp.arange(q + 1, dtype=F32)[None, :, None]
    mag = jnp.exp((ar * dt)[:, None, :] * steps)
    ang = (ai * dt)[:, None, :] * steps
    pr, pi = mag * jnp.cos(ang), mag * jnp.sin(ang)
    x, y = pr[:, 1] - 1.0, pi[:, 1]
    den = ar * ar + ai * ai
    cr, ci = (x * ar + y * ai) / den, (y * ar - x * ai) / den
    br, bi = b_re.astype(F32), b_im.astype(F32)
    bbr = cr[..., None] * br - ci[..., None] * bi
    bbi = cr[..., None] * bi + ci[..., None] * br
    cre, cim = c_re.astype(F32), c_im.astype(F32)
    cpr = cre[:, None] * pr[:, :, None, :] - cim[:, None] * pi[:, :, None, :]
    cpi = cre[:, None] * pi[:, :, None, :] + cim[:, None] * pr[:, :, None, :]
    kj = (jnp.einsum("gjcp,gpi->gjci", cpr[:, :q], bbr, precision=hp)
          - jnp.einsum("gjcp,gpi->gjci", cpi[:, :q], bbi, precision=hp))
    kj = kj.at[:, 0].add(jax.vmap(jnp.diag)(d_skip.astype(F32)))
    s_idx = np.arange(q)[:, None]
    t_idx = np.arange(q)[None, :]
    toe = kj[:, np.clip(t_idx - s_idx, 0, q - 1)]
    toe = jnp.where((t_idx >= s_idx)[None, :, :, None, None], toe, 0.0)
    toe = jnp.transpose(toe, (0, 1, 4, 2, 3))
    rev = q - 1 - np.arange(q)
    pr_s, pi_s = pr[:, rev][..., None], pi[:, rev][..., None]
    st_re = jnp.swapaxes(pr_s * bbr[:, None] - pi_s * bbi[:, None], 2, 3)
    st_im = jnp.swapaxes(pr_s * bbi[:, None] + pi_s * bbr[:, None], 2, 3)
    wst = jnp.stack([st_re, st_im], axis=3)
    cl = jnp.stack([cpr[:, 1:], -cpi[:, 1:]], axis=1)
    cl = jnp.transpose(cl, (0, 1, 4, 2, 3))

    def tile_major(w, lead):
        w = w.reshape((nt, g8) + lead + (LANES,))
        return jnp.swapaxes(w, 1, 2).reshape(nt, SSM_K, LANES)

    toe_c = tile_major(toe.reshape(SSM_GROUPS, q, SSM_GROUP_CH, LANES), (q, SSM_GROUP_CH))
    wst_c = tile_major(wst.reshape(SSM_GROUPS, q, SSM_GROUP_CH, LANES), (q, SSM_GROUP_CH))
    out_c = tile_major(cl.reshape(SSM_GROUPS, 2, SSM_STATE, LANES), (2, SSM_STATE))
    cols = np.arange(SSM_K)
    rep_tc = (np.arange(LANES)[:, None] == ((cols // LANES) * SSM_GROUP_CH + cols % SSM_GROUP_CH)[None, :])
    rep_rp = (np.arange(LANES)[:, None] == ((cols // SSM_ST) * SSM_STATE + cols % SSM_STATE)[None, :])
    wtoe = _block_diag_expand(toe_c, SSM_GROUP_CH, SSM_GROUP_CH, rep_tc)
    wst = _block_diag_expand(wst_c, SSM_GROUP_CH, SSM_STATE, rep_rp)
    wout = _block_diag_expand(out_c, SSM_STATE, SSM_GROUP_CH, rep_tc)
    a_rows = jnp.stack([pr[:, q].reshape(nt, SSM_ST), pi[:, q].reshape(nt, SSM_ST)], axis=1)
    return wtoe, wst, wout, a_rows


def _prepare_params(norm_mix, w_in, q_a_norm, kv_a_norm, w_q_b, w_kv_b, q_norm, k_norm, w_o_mla,
                    ssm_a_re, ssm_a_im, ssm_log_dt, ssm_b_re, ssm_b_im, ssm_c_re, ssm_c_im, ssm_d,
                    w_glu, b_glu, w_o_ssm, w_out, norm_mlp, w_up, w_down):
    o_q = SSM_WIDTH
    o_kv = o_q + Q_LORA
    o_kp = o_kv + KV_LORA
    o_gs = o_kp + QK_ROPE
    w_kpe = w_in[:, o_kp:o_gs]
    w_kpr = _rot_cols(w_kpe)
    w_in2 = jnp.concatenate([w_in[:, :o_kp], w_kpe, w_kpe, w_kpr, w_kpr, w_in[:, o_gs:]], axis=1).astype(BF16)

    wq = w_q_b.reshape(Q_LORA, N_HEADS, QK_HEAD)
    main, aux, gqm, gqa = [], [], [], []
    qscale = (QK_HEAD ** -0.5) * math.log2(math.e)
    g_nope, g_rope = q_norm[:QK_NOPE] * qscale, q_norm[QK_NOPE:] * qscale
    for p in range(N_PAIRS):
        a, b = 2 * p, 2 * p + 1
        main += [wq[:, a, :QK_NOPE], wq[:, b, :QK_NOPE], wq[:, a, QK_NOPE:], wq[:, b, QK_NOPE:]]
        aux += [_rot_cols(wq[:, a, QK_NOPE:]), _rot_cols(wq[:, b, QK_NOPE:])]
        gqm += [g_nope, g_nope, g_rope, g_rope]
        gqa += [_swap_halves(g_rope)] * 2
    wq2 = jnp.concatenate(main + aux, axis=1).astype(BF16)

    wkv = w_kv_b.reshape(KV_LORA, N_HEADS, QK_NOPE + V_HEAD)
    wkv2 = jnp.concatenate([wkv[:, :, :QK_NOPE].reshape(KV_LORA, -1),
                            wkv[:, :, QK_NOPE:].reshape(KV_LORA, -1)], axis=1).astype(BF16)
    gk_rope = k_norm[QK_NOPE:]
    wtoe, wst, wout_ssm, a_rows = _ssm_params(ssm_a_re, ssm_a_im, ssm_log_dt, ssm_b_re, ssm_b_im,
                                              ssm_c_re, ssm_c_im, ssm_d)
    row = lambda v: v.astype(F32).reshape(1, -1)
    return {
        "nmix": row(norm_mix), "w_in": w_in2, "qan": row(q_a_norm), "kvan": row(kv_a_norm),
        "wq": wq2, "wkv": wkv2,
        "gqm": row(jnp.concatenate(gqm)), "gqa": row(jnp.concatenate(gqa)),
        "gkn": row(jnp.tile(k_norm[:QK_NOPE], N_HEADS)),
        "gkx": row(jnp.tile(gk_rope, 2)), "gkr": row(jnp.tile(_swap_halves(gk_rope), 2)),
        "ssm_wtoe": wtoe, "ssm_wst": wst, "ssm_wout": wout_ssm, "ssm_a": a_rows,
        "w_glu": w_glu.astype(BF16), "b_glu": row(b_glu), "w_o_ssm": w_o_ssm.astype(BF16),
        "w_o_mla": w_o_mla.astype(BF16), "w_out": w_out.astype(BF16), "nmlp": row(norm_mlp),
        "w_up": w_up.astype(BF16), "w_down": w_down.astype(BF16),
    }


def _layer(h, cos, sin, prm):
    batch, seq, _ = h.shape
    t = batch * seq
    x2d = h.reshape(t, D_MODEL)
    u, q, k, v, sgs, sgm = _in_proj(x2d, cos, sin, prm, batch, seq)
    n_rows = (seq // SSM_CHUNK) * batch
    y = _ssm(u.reshape(n_rows, SSM_CHUNK * SSM_WIDTH), prm, batch, seq)
    attn = _attention(q.reshape(batch, seq, Q_MAIN), k, v.reshape(batch, seq, N_HEADS * V_HEAD), batch, seq)
    out = _post(x2d, y.reshape(t, SSM_WIDTH), attn.reshape(t, N_HEADS * V_HEAD), sgs, sgm, prm)
    return out.reshape(batch, seq, D_MODEL)


def kernel(x, positions, norm_mix, w_in, q_a_norm, kv_a_norm, w_q_b, w_kv_b, q_norm, k_norm, w_o_mla,
           ssm_a_re, ssm_a_im, ssm_log_dt, ssm_b_re, ssm_b_im, ssm_c_re, ssm_c_im, ssm_d,
           w_glu, b_glu, w_o_ssm, w_out, norm_mlp, w_up, w_down):
    params = (norm_mix, w_in, q_a_norm, kv_a_norm, w_q_b, w_kv_b, q_norm, k_norm, w_o_mla,
              ssm_a_re, ssm_a_im, ssm_log_dt, ssm_b_re, ssm_b_im, ssm_c_re, ssm_c_im, ssm_d,
              w_glu, b_glu, w_o_ssm, w_out, norm_mlp, w_up, w_down)
    cos, sin = _rope_tables(positions)
    h = x
    for layer in range(norm_mix.shape[0]):
        h = _layer(h, cos, sin, _prepare_params(*[p[layer] for p in params]))
    return h
```

```python
import functools
import math

import numpy as np
import jax
import jax.numpy as jnp
from jax import lax
from jax.experimental import pallas as pl
from jax.experimental.pallas import tpu as pltpu

D_MODEL = 1024
SSM_GROUPS = 32
SSM_GROUP_CH = 16
SSM_WIDTH = SSM_GROUPS * SSM_GROUP_CH
SSM_STATE = 64
N_HEADS = 8
N_PAIRS = N_HEADS // 2
QK_NOPE = 128
QK_ROPE = 64
QK_HEAD = QK_NOPE + QK_ROPE
V_HEAD = 128
Q_LORA = 384
KV_LORA = 256
ROPE_THETA = 10000.0
D_FF = 4 * D_MODEL
EPS = 1e-6

LANES = 128
SUBLANES = 8
SSM_CHUNK = SUBLANES
GROUPS_PER_TILE = LANES // SSM_GROUP_CH
SSM_TILES = SSM_WIDTH // LANES
SSM_K = SSM_CHUNK * LANES
SSM_ST = GROUPS_PER_TILE * SSM_STATE
SSM_ROWS = 1024
HEAD_PAD = 256
PAIR_W = 2 * QK_NOPE + LANES
Q_MAIN = N_PAIRS * PAIR_W
Q_AUX = N_PAIRS * LANES
VMEM_LIMIT = 56 * 1024 * 1024

TM_PROJ = 512
TM_POST = 512
TQ = 512
TK = 512
ATTN_ROWS = 64
ATTN_HEADS = 4
FF_CHUNK = 1024

F32 = jnp.float32
BF16 = jnp.bfloat16


def _dot(a, b):
    return jnp.dot(a, b, preferred_element_type=F32)


def _rms(x, gain):
    ms = jnp.mean(x * x, axis=-1, keepdims=True)
    return x * lax.rsqrt(ms + EPS) * gain


def _rope_kernel(pos_ref, freq_ref, cos_ref, sin_ref):
    ang = freq_ref[...] * pos_ref[0].astype(F32)
    reps = LANES // ang.shape[0]
    cos_ref[...] = jnp.concatenate([jnp.cos(ang)] * reps, axis=0).T
    sin_ref[...] = jnp.concatenate([jnp.sin(ang)] * reps, axis=0).T


def _rope_tables(positions):
    b, l = positions.shape
    half = QK_ROPE // 2
    inv_freq = ROPE_THETA ** (-jnp.arange(half, dtype=F32) / half)
    out = jax.ShapeDtypeStruct((b * l, LANES), F32)
    return pl.pallas_call(
        _rope_kernel,
        grid=(b,),
        in_specs=[pl.BlockSpec((1, 1, l), lambda i: (i, 0, 0)),
                  pl.BlockSpec((half, 1), lambda i: (0, 0))],
        out_specs=[pl.BlockSpec((l, LANES), lambda i: (i, 0))] * 2,
        out_shape=[out, out],
        name="rope_tables",
    )(positions.reshape(b, 1, l), inv_freq.reshape(half, 1))


def _in_proj_kernel(x_ref, cos_ref, sin_ref, nmix_ref, w_in_ref, qan_ref, kvan_ref, wq_ref, wkv_ref,
                    gqm_ref, gqa_ref, gkn_ref, gkx_ref, gkr_ref,
                    u_ref, q_ref, k_ref, v_ref, sgs_ref, sgm_ref):
    o_q = SSM_WIDTH
    o_kv = Q_LORA
    o_kp = o_kv + KV_LORA
    o_kr = o_kp + LANES
    o_gs = o_q + o_kr + LANES
    o_gm = o_gs + D_MODEL
    tm = x_ref.shape[0]

    xn = _rms(x_ref[...], nmix_ref[...]).astype(BF16)
    cos = cos_ref[...]
    sin = sin_ref[...]

    u_ref[...] = _dot(xn, w_in_ref[:, 0:o_q]).reshape(u_ref.shape)
    sgs_ref[...] = jax.nn.sigmoid(_dot(xn, w_in_ref[:, o_gs:o_gm])).astype(BF16)
    sgm_ref[...] = jax.nn.sigmoid(_dot(xn, w_in_ref[:, o_gm:o_gm + D_MODEL])).astype(BF16)
    lat = _dot(xn, w_in_ref[:, o_q:o_gs])

    first_half = lax.broadcasted_iota(jnp.int32, (tm, LANES), 1) < QK_ROPE

    def inv_rms(sq):
        return lax.rsqrt(jnp.sum(sq, axis=-1, keepdims=True) * (1.0 / QK_HEAD) + EPS)

    qn = _rms(lat[:, 0:o_kv], qan_ref[...]).astype(BF16)
    qraw = _dot(qn, wq_ref[...])
    for p in range(N_PAIRS):
        c0 = p * PAIR_W
        na, nb = qraw[:, c0:c0 + QK_NOPE], qraw[:, c0 + QK_NOPE:c0 + 2 * QK_NOPE]
        rp = qraw[:, c0 + 2 * QK_NOPE:c0 + PAIR_W]
        rr = qraw[:, Q_MAIN + p * LANES:Q_MAIN + (p + 1) * LANES]
        rp2 = rp * rp
        inv_a = inv_rms(na * na + jnp.where(first_half, rp2, 0.0))
        inv_b = inv_rms(nb * nb + jnp.where(first_half, 0.0, rp2))
        inv_r = jnp.where(first_half, inv_a, inv_b)
        q_ref[:, c0:c0 + QK_NOPE] = (na * inv_a * gqm_ref[:, c0:c0 + QK_NOPE]).astype(BF16)
        q_ref[:, c0 + QK_NOPE:c0 + 2 * QK_NOPE] = (nb * inv_b * gqm_ref[:, c0 + QK_NOPE:c0 + 2 * QK_NOPE]).astype(BF16)
        rope = (rp * inv_r * gqm_ref[:, c0 + 2 * QK_NOPE:c0 + PAIR_W]) * cos \
            + (rr * inv_r * gqa_ref[:, p * LANES:(p + 1) * LANES]) * sin
        q_ref[:, c0 + 2 * QK_NOPE:c0 + PAIR_W] = rope.astype(BF16)

    ckv = _rms(lat[:, o_kv:o_kp], kvan_ref[...]).astype(BF16)
    kvraw = _dot(ckv, wkv_ref[...])
    v_ref[...] = kvraw[:, N_HEADS * QK_NOPE:].astype(BF16)
    kpe = lat[:, o_kp:o_kr]
    kpe_rot = lat[:, o_kr:o_kr + LANES]
    kpe_sq = kpe * kpe * 0.5
    krope = (kpe * gkx_ref[...]) * cos + (kpe_rot * gkr_ref[...]) * sin
    for h in range(N_HEADS):
        kn = kvraw[:, h * QK_NOPE:(h + 1) * QK_NOPE]
        inv = inv_rms(kn * kn + kpe_sq)
        k_ref[0, h, :, 0:QK_NOPE] = (kn * inv * gkn_ref[:, h * QK_NOPE:(h + 1) * QK_NOPE]).astype(BF16)
        own_half = first_half if h % 2 == 0 else jnp.logical_not(first_half)
        k_ref[0, h, :, QK_NOPE:HEAD_PAD] = jnp.where(own_half, krope * inv, 0.0).astype(BF16)


def _const_spec(shape):
    nd = len(shape)
    return pl.BlockSpec(shape, lambda *_: (0,) * nd, pipeline_mode=pl.Buffered(1))


def _chunk_rows_spec(tm, nlt):
    return pl.BlockSpec((tm // SSM_CHUNK, None, SSM_CHUNK, SSM_WIDTH), lambda i: (i % nlt, i // nlt, 0, 0))


def _in_proj(x2d, cos, sin, prm, batch, seq):
    t = x2d.shape[0]
    tm = TM_PROJ
    nlt = seq // tm
    row = lambda w: pl.BlockSpec((tm, w), lambda i: (i, 0))
    consts = [prm["nmix"], prm["w_in"], prm["qan"], prm["kvan"], prm["wq"], prm["wkv"],
              prm["gqm"], prm["gqa"], prm["gkn"], prm["gkx"], prm["gkr"]]
    out_shape = [
        jax.ShapeDtypeStruct((seq // SSM_CHUNK, batch, SSM_CHUNK, SSM_WIDTH), F32),
        jax.ShapeDtypeStruct((t, Q_MAIN), BF16),
        jax.ShapeDtypeStruct((batch, N_HEADS, seq, HEAD_PAD), BF16),
        jax.ShapeDtypeStruct((t, N_HEADS * V_HEAD), BF16),
        jax.ShapeDtypeStruct((t, D_MODEL), BF16),
        jax.ShapeDtypeStruct((t, D_MODEL), BF16),
    ]
    out_specs = [
        _chunk_rows_spec(tm, nlt), row(Q_MAIN),
        pl.BlockSpec((1, N_HEADS, tm, HEAD_PAD), lambda i: (i // nlt, 0, i % nlt, 0)),
        row(N_HEADS * V_HEAD), row(D_MODEL), row(D_MODEL),
    ]
    return pl.pallas_call(
        _in_proj_kernel,
        grid=(t // tm,),
        in_specs=[row(D_MODEL), row(LANES), row(LANES)] + [_const_spec(c.shape) for c in consts],
        out_specs=out_specs,
        out_shape=out_shape,
        compiler_params=pltpu.CompilerParams(dimension_semantics=("arbitrary",),
                                             vmem_limit_bytes=VMEM_LIMIT),
        name="in_proj",
    )(x2d, cos, sin, *consts)


def _ssm_kernel(*refs):
    nq = SSM_CHUNK
    u_refs = refs[0:nq]
    wtoe_ref, wst_ref, wout_ref, a_ref, y_ref, s_sc, h_sc, carry_sc, y_sc = refs[nq:]
    rows = s_sc.shape[0]
    n_chunks = rows // SUBLANES

    @pl.when(pl.program_id(1) == 0)
    def _():
        carry_sc[...] = jnp.zeros(carry_sc.shape, F32)

    lhs = jnp.concatenate([u[...] for u in u_refs], axis=1).astype(BF16)
    s_sc[...] = _dot(lhs, wst_ref[0])
    a_re = jnp.broadcast_to(a_ref[0, 0:1, :], (SUBLANES, SSM_ST))
    a_im = jnp.broadcast_to(a_ref[0, 1:2, :], (SUBLANES, SSM_ST))

    def body(k, carry):
        h_re, h_im = carry
        r0 = pl.multiple_of(k * SUBLANES, SUBLANES)
        h_sc[pl.ds(r0, SUBLANES), 0:SSM_ST] = h_re
        h_sc[pl.ds(r0, SUBLANES), SSM_ST:2 * SSM_ST] = h_im
        s_re = s_sc[pl.ds(r0, SUBLANES), 0:SSM_ST]
        s_im = s_sc[pl.ds(r0, SUBLANES), SSM_ST:2 * SSM_ST]
        return (a_re * h_re - a_im * h_im + s_re, a_re * h_im + a_im * h_re + s_im)

    h_re, h_im = lax.fori_loop(0, rows // SUBLANES, body,
                               (carry_sc[:, 0:SSM_ST], carry_sc[:, SSM_ST:2 * SSM_ST]), unroll=4)
    carry_sc[:, 0:SSM_ST] = h_re
    carry_sc[:, SSM_ST:2 * SSM_ST] = h_im

    hb = h_sc[...].astype(BF16)
    for tt in range(nq // 2):
        kk = (2 * tt + 2) * LANES
        c0, c1 = 2 * tt * LANES, (2 * tt + 2) * LANES
        y = _dot(lhs[:, 0:kk], wtoe_ref[0, 0:kk, c0:c1]) + _dot(hb, wout_ref[0, :, c0:c1])
        y_sc[0] = y[:, 0:LANES]
        y_sc[1] = y[:, LANES:2 * LANES]
        for j in range(2):
            for b in range(y_ref.shape[0]):
                y_ref[b, pl.ds(2 * tt + j, n_chunks, stride=SSM_CHUNK), :] = \
                    y_sc[j, pl.ds(b, n_chunks, stride=y_ref.shape[0]), :]


def _ssm(u_rows, prm, batch, seq):
    n_rows = u_rows.shape[0]
    rows = SSM_ROWS
    assert batch == SUBLANES, "one scan step works on a full vreg of batch rows"
    col = lambda s: pl.BlockSpec((rows, LANES), lambda g, r, s=s: (r, s * SSM_TILES + g))
    wspec = lambda: pl.BlockSpec((1, SSM_K, SSM_K), lambda g, r: (g, 0, 0))
    return pl.pallas_call(
        _ssm_kernel,
        grid=(SSM_TILES, n_rows // rows),
        in_specs=[col(s) for s in range(SSM_CHUNK)] + [wspec(), wspec(), wspec(),
                  pl.BlockSpec((1, 2, SSM_ST), lambda g, r: (g, 0, 0))],
        out_specs=pl.BlockSpec((batch, rows, LANES), lambda g, r: (0, r, g)),
        out_shape=jax.ShapeDtypeStruct((batch, seq, SSM_WIDTH), F32),
        scratch_shapes=[pltpu.VMEM((rows, 2 * SSM_ST), F32), pltpu.VMEM((rows, 2 * SSM_ST), F32),
                        pltpu.VMEM((SUBLANES, 2 * SSM_ST), F32), pltpu.VMEM((2, rows, LANES), F32)],
        compiler_params=pltpu.CompilerParams(dimension_semantics=("arbitrary", "arbitrary"),
                                             vmem_limit_bytes=VMEM_LIMIT),
        name="ssm_scan",
    )(*([u_rows] * SSM_CHUNK), prm["ssm_wtoe"], prm["ssm_wst"], prm["ssm_wout"], prm["ssm_a"])


def _attn_kernel(q_ref, k_ref, v_ref, o_ref, *scratch):
    nh = ATTN_HEADS
    m_sc, acc_sc, s_sc, p_sc, alpha_sc = (scratch[nh * n:nh * (n + 1)] for n in range(5))
    i = pl.program_id(2)
    ones = jnp.ones((TK, LANES), BF16)
    nt = (((1,), (1,)), ((), ()))

    def stage_q(kv0):
        for h in range(nh):
            pair = (h // 2) * PAIR_W
            q = jnp.concatenate([q_ref[0, :, pair + (h % 2) * QK_NOPE:pair + (h % 2 + 1) * QK_NOPE],
                                 q_ref[0, :, pair + 2 * QK_NOPE:pair + PAIR_W]], axis=1)
            k = k_ref[0, h, pl.ds(kv0, TK), :]
            s_sc[h][...] = lax.dot_general(q, k, nt, preferred_element_type=F32)

    def stage_p(kv0):
        for h in range(nh):
            v_ext = jnp.concatenate([v_ref[0, pl.ds(kv0, TK), h * V_HEAD:(h + 1) * V_HEAD], ones], axis=1)
            acc_sc[h][...] = acc_sc[h][...] * jnp.tile(alpha_sc[h][...], (1, 2)) + _dot(p_sc[h][...], v_ext)

    def stage_s(first):
        for h in range(nh):
            for r0 in range(0, TQ, ATTN_ROWS):
                rows = slice(r0, r0 + ATTN_ROWS)
                s = s_sc[h][rows, :]
                if first:
                    r = lax.broadcasted_iota(jnp.int32, (ATTN_ROWS, TK), 0) + r0
                    c = lax.broadcasted_iota(jnp.int32, (ATTN_ROWS, TK), 1)
                    s = jnp.where(c <= r, s, -jnp.inf)
                m_prev = m_sc[h][rows, :]
                m_new = jnp.maximum(m_prev, jnp.max(s, axis=-1, keepdims=True))
                alpha_sc[h][rows, :] = jnp.exp2(m_prev - m_new)
                p_sc[h][rows, :] = jnp.exp2(s - jnp.tile(m_new, (1, TK // LANES))).astype(BF16)
                m_sc[h][rows, :] = m_new

    blk = lambda j: pl.multiple_of(j * TK, TK)
    diag = blk(i)
    for h in range(nh):
        m_sc[h][...] = jnp.full((TQ, LANES), -jnp.inf, F32)
        acc_sc[h][...] = jnp.zeros((TQ, 2 * V_HEAD), F32)
    stage_q(diag)
    stage_s(True)

    @pl.when(i > 0)
    def _():
        stage_q(blk(0))

        def body(t, carry):
            stage_p(pl.multiple_of(jnp.where(t == 1, diag, blk(t - 2)), TK))
            stage_s(False)
            stage_q(blk(t))
            return carry

        lax.fori_loop(1, i, body, 0)
        stage_p(pl.multiple_of(jnp.where(i == 1, diag, blk(i - 2)), TK))
        stage_s(False)

    stage_p(blk(jnp.maximum(i - 1, 0)))
    for h in range(nh):
        acc = acc_sc[h][...]
        o_ref[0, :, h * V_HEAD:(h + 1) * V_HEAD] = (acc[:, 0:V_HEAD] / acc[:, V_HEAD:2 * V_HEAD]).astype(BF16)


def _attention(q, k, v, batch, seq):
    assert TQ == TK
    nh = ATTN_HEADS
    per_head = lambda shape, dt: [pltpu.VMEM(shape, dt)] * nh
    return pl.pallas_call(
        _attn_kernel,
        grid=(batch, N_HEADS // nh, seq // TQ),
        in_specs=[pl.BlockSpec((1, TQ, nh // 2 * PAIR_W), lambda b, g, i: (b, i, g)),
                  pl.BlockSpec((1, nh, seq, HEAD_PAD), lambda b, g, i: (b, g, 0, 0)),
                  pl.BlockSpec((1, seq, nh * V_HEAD), lambda b, g, i: (b, 0, g))],
        out_specs=pl.BlockSpec((1, TQ, nh * V_HEAD), lambda b, g, i: (b, i, g)),
        out_shape=jax.ShapeDtypeStruct((batch, seq, N_HEADS * V_HEAD), BF16),
        scratch_shapes=per_head((TQ, LANES), F32) + per_head((TQ, 2 * V_HEAD), F32)
                       + per_head((TQ, TK), F32) + per_head((TQ, TK), BF16) + per_head((TQ, LANES), F32),
        compiler_params=pltpu.CompilerParams(dimension_semantics=("arbitrary",) * 3,
                                             vmem_limit_bytes=VMEM_LIMIT),
        name="mla_attention",
    )(q, k, v)


def _post_kernel(x_ref, y_ref, attn_ref, sgs_ref, sgm_ref, wglu_ref, bglu_ref, wos_ref, wom_ref, wout_ref,
                 nmlp_ref, wup_ref, wdown_ref, o_ref):
    z = jax.nn.gelu(y_ref[...])
    z = z * jax.nn.sigmoid(_dot(z.astype(BF16), wglu_ref[...]) + bglu_ref[...])
    y_ssm = _dot(z.astype(BF16), wos_ref[...])
    y_mla = _dot(attn_ref[...], wom_ref[...])
    mixed = sgs_ref[...].astype(F32) * y_ssm + sgm_ref[...].astype(F32) * y_mla
    h = x_ref[...] + _dot(mixed.astype(BF16), wout_ref[...])
    hn = _rms(h, nmlp_ref[...]).astype(BF16)
    acc = h
    for c in range(D_FF // FF_CHUNK):
        up = _dot(hn, wup_ref[:, c * FF_CHUNK:(c + 1) * FF_CHUNK])
        hid = jnp.square(jnp.maximum(up, 0.0)).astype(BF16)
        acc = acc + _dot(hid, wdown_ref[c * FF_CHUNK:(c + 1) * FF_CHUNK, :])
    o_ref[...] = acc


def _post(x2d, y2d, attn2d, sgs, sgm, prm):
    t = x2d.shape[0]
    tm = TM_POST
    row = lambda w: pl.BlockSpec((tm, w), lambda i: (i, 0))
    consts = [prm["w_glu"], prm["b_glu"], prm["w_o_ssm"], prm["w_o_mla"], prm["w_out"],
              prm["nmlp"], prm["w_up"], prm["w_down"]]
    return pl.pallas_call(
        _post_kernel,
        grid=(t // tm,),
        in_specs=[row(D_MODEL), row(SSM_WIDTH), row(D_MODEL), row(D_MODEL), row(D_MODEL)]
                 + [_const_spec(c.shape) for c in consts],
        out_specs=row(D_MODEL),
        out_shape=jax.ShapeDtypeStruct((t, D_MODEL), F32),
        compiler_params=pltpu.CompilerParams(dimension_semantics=("arbitrary",),
                                             vmem_limit_bytes=VMEM_LIMIT),
        name="merge_mlp",
    )(x2d, y2d, attn2d, sgs, sgm, *consts)


def _rot_cols(w):
    half = w.shape[-1] // 2
    return jnp.concatenate([-w[..., half:], w[..., :half]], axis=-1)


def _swap_halves(g):
    half = g.shape[-1] // 2
    return jnp.concatenate([g[..., half:], g[..., :half]], axis=-1)


def _block_diag_expand(compact, row_div, col_div, rep):
    r = (np.arange(SSM_K) // row_div) % GROUPS_PER_TILE
    c = (np.arange(SSM_K) // col_div) % GROUPS_PER_TILE
    mask = jnp.asarray(r[:, None] == c[None, :], F32)
    wide = jnp.einsum("grk,kc->grc", compact, jnp.asarray(rep, F32))
    return (wide * mask).astype(BF16)


def _ssm_params(a_re, a_im, log_dt, b_re, b_im, c_re, c_im, d_skip):
    q, g8, nt = SSM_CHUNK, GROUPS_PER_TILE, SSM_TILES
    hp = lax.Precision.HIGHEST
    ar, ai = a_re.astype(F32), a_im.astype(F32)
    dt = jnp.exp(log_dt.astype(F32))[:, None]
    steps = jnp.arange(q + 1, dtype=F32)[None, :, None]
    mag = jnp.exp((ar * dt)[:, None, :] * steps)
    ang = (ai * dt)[:, None, :] * steps
    pr, pi = mag * jnp.cos(ang), mag * jnp.sin(ang)
    x, y = pr[:, 1] - 1.0, pi[:, 1]
    den = ar * ar + ai * ai
    cr, ci = (x * ar + y * ai) / den, (y * ar - x * ai) / den
    br, bi = b_re.astype(F32), b_im.astype(F32)
    bbr = cr[..., None] * br - ci[..., None] * bi
    bbi = cr[..., None] * bi + ci[..., None] * br
    cre, cim = c_re.astype(F32), c_im.astype(F32)
    cpr = cre[:, None] * pr[:, :, None, :] - cim[:, None] * pi[:, :, None, :]
    cpi = cre[:, None] * pi[:, :, None, :] + cim[:, None] * pr[:, :, None, :]
    kj = (jnp.einsum("gjcp,gpi->gjci", cpr[:, :q], bbr, precision=hp)
          - jnp.einsum("gjcp,gpi->gjci", cpi[:, :q], bbi, precision=hp))
    kj = kj.at[:, 0].add(jax.vmap(jnp.diag)(d_skip.astype(F32)))
    lag_rows = jnp.transpose(kj, (0, 3, 1, 2)).reshape(SSM_GROUPS, SSM_GROUP_CH, LANES)
    toe = jnp.stack([jnp.pad(lag_rows, ((0, 0), (0, 0), (s * SSM_GROUP_CH, 0)))[:, :, :LANES]
                     for s in range(q)], axis=1)
    rev = q - 1 - np.arange(q)
    pr_s, pi_s = pr[:, rev][..., None], pi[:, rev][..., None]
    st_re = jnp.swapaxes(pr_s * bbr[:, None] - pi_s * bbi[:, None], 2, 3)
    st_im = jnp.swapaxes(pr_s * bbi[:, None] + pi_s * bbr[:, None], 2, 3)
    wst = jnp.stack([st_re, st_im], axis=3)
    cl = jnp.stack([cpr[:, 1:], -cpi[:, 1:]], axis=1)
    cl = jnp.transpose(cl, (0, 1, 4, 2, 3))

    def tile_major(w, lead):
        w = w.reshape((nt, g8) + lead + (LANES,))
        return jnp.swapaxes(w, 1, 2).reshape(nt, SSM_K, LANES)

    toe_c = tile_major(toe.reshape(SSM_GROUPS, q, SSM_GROUP_CH, LANES), (q, SSM_GROUP_CH))
    wst_c = tile_major(wst.reshape(SSM_GROUPS, q, SSM_GROUP_CH, LANES), (q, SSM_GROUP_CH))
    out_c = tile_major(cl.reshape(SSM_GROUPS, 2, SSM_STATE, LANES), (2, SSM_STATE))
    cols = np.arange(SSM_K)
    rep_tc = (np.arange(LANES)[:, None] == ((cols // LANES) * SSM_GROUP_CH + cols % SSM_GROUP_CH)[None, :])
    rep_rp = (np.arange(LANES)[:, None] == ((cols // SSM_ST) * SSM_STATE + cols % SSM_STATE)[None, :])
    wtoe = _block_diag_expand(toe_c, SSM_GROUP_CH, SSM_GROUP_CH, rep_tc)
    wst = _block_diag_expand(wst_c, SSM_GROUP_CH, SSM_STATE, rep_rp)
    wout = _block_diag_expand(out_c, SSM_STATE, SSM_GROUP_CH, rep_tc)
    a_rows = jnp.stack([pr[:, q].reshape(nt, SSM_ST), pi[:, q].reshape(nt, SSM_ST)], axis=1)
    return wtoe, wst, wout, a_rows


def _prepare_params(norm_mix, w_in, q_a_norm, kv_a_norm, w_q_b, w_kv_b, q_norm, k_norm, w_o_mla,
                    ssm_a_re, ssm_a_im, ssm_log_dt, ssm_b_re, ssm_b_im, ssm_c_re, ssm_c_im, ssm_d,
                    w_glu, b_glu, w_o_ssm, w_out, norm_mlp, w_up, w_down):
    o_q = SSM_WIDTH
    o_kv = o_q + Q_LORA
    o_kp = o_kv + KV_LORA
    o_gs = o_kp + QK_ROPE
    w_kpe = w_in[:, o_kp:o_gs]
    w_kpr = _rot_cols(w_kpe)
    w_in2 = jnp.concatenate([w_in[:, :o_kp], w_kpe, w_kpe, w_kpr, w_kpr, w_in[:, o_gs:]], axis=1).astype(BF16)

    wq = w_q_b.reshape(Q_LORA, N_HEADS, QK_HEAD)
    main, aux, gqm, gqa = [], [], [], []
    qscale = (QK_HEAD ** -0.5) * math.log2(math.e)
    g_nope, g_rope = q_norm[:QK_NOPE] * qscale, q_norm[QK_NOPE:] * qscale
    for p in range(N_PAIRS):
        a, b = 2 * p, 2 * p + 1
        main += [wq[:, a, :QK_NOPE], wq[:, b, :QK_NOPE], wq[:, a, QK_NOPE:], wq[:, b, QK_NOPE:]]
        aux += [_rot_cols(wq[:, a, QK_NOPE:]), _rot_cols(wq[:, b, QK_NOPE:])]
        gqm += [g_nope, g_nope, g_rope, g_rope]
        gqa += [_swap_halves(g_rope)] * 2
    wq2 = jnp.concatenate(main + aux, axis=1).astype(BF16)

    wkv = w_kv_b.reshape(KV_LORA, N_HEADS, QK_NOPE + V_HEAD)
    wkv2 = jnp.concatenate([wkv[:, :, :QK_NOPE].reshape(KV_LORA, -1),
                            wkv[:, :, QK_NOPE:].reshape(KV_LORA, -1)], axis=1).astype(BF16)
    gk_rope = k_norm[QK_NOPE:]
    wtoe, wst, wout_ssm, a_rows = _ssm_params(ssm_a_re, ssm_a_im, ssm_log_dt, ssm_b_re, ssm_b_im,
                                              ssm_c_re, ssm_c_im, ssm_d)
    row = lambda v: v.astype(F32).reshape(1, -1)
    return {
        "nmix": row(norm_mix), "w_in": w_in2, "qan": row(q_a_norm), "kvan": row(kv_a_norm),
        "wq": wq2, "wkv": wkv2,
        "gqm": row(jnp.concatenate(gqm)), "gqa": row(jnp.concatenate(gqa)),
        "gkn": row(jnp.tile(k_norm[:QK_NOPE], N_HEADS)),
        "gkx": row(jnp.tile(gk_rope, 2)), "gkr": row(jnp.tile(_swap_halves(gk_rope), 2)),
        "ssm_wtoe": wtoe, "ssm_wst": wst, "ssm_wout": wout_ssm, "ssm_a": a_rows,
        "w_glu": w_glu.astype(BF16), "b_glu": row(b_glu), "w_o_ssm": w_o_ssm.astype(BF16),
        "w_o_mla": w_o_mla.astype(BF16), "w_out": w_out.astype(BF16), "nmlp": row(norm_mlp),
        "w_up": w_up.astype(BF16), "w_down": w_down.astype(BF16),
    }


def _layer(h, cos, sin, prm):
    batch, seq, _ = h.shape
    t = batch * seq
    x2d = h.reshape(t, D_MODEL)
    u, q, k, v, sgs, sgm = _in_proj(x2d, cos, sin, prm, batch, seq)
    n_rows = (seq // SSM_CHUNK) * batch
    y = _ssm(u.reshape(n_rows, SSM_CHUNK * SSM_WIDTH), prm, batch, seq)
    attn = _attention(q.reshape(batch, seq, Q_MAIN), k, v.reshape(batch, seq, N_HEADS * V_HEAD), batch, seq)
    out = _post(x2d, y.reshape(t, SSM_WIDTH), attn.reshape(t, N_HEADS * V_HEAD), sgs, sgm, prm)
    return out.reshape(batch, seq, D_MODEL)


def kernel(x, positions, norm_mix, w_in, q_a_norm, kv_a_norm, w_q_b, w_kv_b, q_norm, k_norm, w_o_mla,
           ssm_a_re, ssm_a_im, ssm_log_dt, ssm_b_re, ssm_b_im, ssm_c_re, ssm_c_im, ssm_d,
           w_glu, b_glu, w_o_ssm, w_out, norm_mlp, w_up, w_down):
    params = (norm_mix, w_in, q_a_norm, kv_a_norm, w_q_b, w_kv_b, q_norm, k_norm, w_o_mla,
              ssm_a_re, ssm_a_im, ssm_log_dt, ssm_b_re, ssm_b_im, ssm_c_re, ssm_c_im, ssm_d,
              w_glu, b_glu, w_o_ssm, w_out, norm_mlp, w_up, w_down)
    cos, sin = _rope_tables(positions)
    h = x
    for layer in range(norm_mix.shape[0]):
        h = _layer(h, cos, sin, _prepare_params(*[p[layer] for p in params]))
    return h
```
